```python
import jax, jax.numpy as jnp
from jax import lax
import numpy as np

D_MODEL = 1024
BATCH = 4
SEQ = 4096
DEPTH = 4

CHUNK = 64
QBLOCK = 128
N_A_LAYERS = DEPTH // 2
N_B_LAYERS = DEPTH - N_A_LAYERS

GDN_HEAD_DIM = 128
GDN_HEADS = D_MODEL // GDN_HEAD_DIM
GDN_WIDTH = GDN_HEADS * GDN_HEAD_DIM
CONV_K = 4
GDN_PROJ = 4 * GDN_WIDTH + 2 * GDN_HEADS

MLA_HEADS = D_MODEL // 128
QK_NOPE = 128
QK_ROPE = 64
QK_HEAD = QK_NOPE + QK_ROPE
V_HEAD = 128
KV_LORA = D_MODEL // 4
Q_LORA = 3 * D_MODEL // 8
ROPE_BASE = 10000.0

D_FF = ((8 * D_MODEL // 3 + 127) // 128) * 128
N_MOD = 9
EPS = 1e-6
MAX_POS_OFFSET = 2048

kernel_name = "hybrid_gdn_mla_yoco_macaron_adaln"


def rms_norm(t, g):
    tf = t.astype(jnp.float32)
    y = tf * lax.rsqrt(jnp.mean(tf * tf, axis=-1, keepdims=True) + EPS)
    return (y * g.astype(jnp.float32)).astype(t.dtype)


def l2_norm(t):
    tf = t.astype(jnp.float32)
    return tf * lax.rsqrt(jnp.sum(tf * tf, axis=-1, keepdims=True) + EPS)


def modulate(t, g, shift, scale):
    return rms_norm(t, g) * (1 + scale[:, None, :]) + shift[:, None, :]


def swiglu(h, w_in, w_out):
    gate, up = jnp.split(h @ w_in, 2, axis=-1)
    return (jax.nn.silu(gate) * up) @ w_out


def rotary(t, positions):
    half = t.shape[-1] // 2
    inv_freq = ROPE_BASE ** (-jnp.arange(half, dtype=jnp.float32) / half)
    ang = positions.astype(jnp.float32)[..., None] * inv_freq
    cos = jnp.cos(ang)[:, :, None, :]
    sin = jnp.sin(ang)[:, :, None, :]
    tf = t.astype(jnp.float32)
    t1, t2 = tf[..., :half], tf[..., half:]
    return jnp.concatenate([t1 * cos - t2 * sin, t2 * cos + t1 * sin], axis=-1).astype(t.dtype)


def causal_depthwise_conv(t, w):
    k = w.shape[0]
    return lax.conv_general_dilated(
        t, w[:, None, :].astype(t.dtype), window_strides=(1,), padding=[(k - 1, 0)],
        dimension_numbers=("NWC", "WIO", "NWC"), feature_group_count=t.shape[-1])


def gated_delta_rule(q, k, v, g, beta):
    b_, s_, h_, dk = q.shape
    dv = v.shape[-1]
    nc = s_ // CHUNK
    q = q.astype(jnp.float32) * (dk ** -0.5)

    def to_chunks(t):
        return t.astype(jnp.float32).reshape(b_, nc, CHUNK, h_, -1).transpose(1, 0, 3, 2, 4)

    qc, kc, vc = to_chunks(q), to_chunks(k), to_chunks(v)
    gc = g.astype(jnp.float32).reshape(b_, nc, CHUNK, h_).transpose(1, 0, 3, 2)
    bc = beta.astype(jnp.float32).reshape(b_, nc, CHUNK, h_).transpose(1, 0, 3, 2)
    gcum = jnp.cumsum(gc, axis=-1)

    idx = jnp.arange(CHUNK)
    incl = idx[:, None] >= idx[None, :]
    strict = idx[:, None] > idx[None, :]
    diff = gcum[..., :, None] - gcum[..., None, :]
    decay = jnp.where(incl, jnp.exp(jnp.where(incl, diff, 0.0)), 0.0)

    kb = kc * bc[..., None]
    lower = jnp.where(strict, jnp.einsum("nbhid,nbhjd->nbhij", kb, kc) * decay, 0.0)
    eye = jnp.eye(CHUNK, dtype=jnp.float32)
    rhs = jnp.concatenate([vc * bc[..., None], kb * jnp.exp(gcum)[..., None]], axis=-1)
    sol = lax.linalg.triangular_solve(eye + lower, rhs, left_side=True, lower=True)
    u, w = sol[..., :dv], sol[..., dv:]
    attn_intra = jnp.where(incl, jnp.einsum("nbhid,nbhjd->nbhij", qc, kc) * decay, 0.0)

    def step(state, inp):
        qi, ki, ui, wi, gi, ai = inp
        v_new = ui - jnp.einsum("bhck,bhkv->bhcv", wi, state)
        o = jnp.einsum("bhck,bhkv->bhcv", qi * jnp.exp(gi)[..., None], state) \
            + jnp.einsum("bhcj,bhjv->bhcv", ai, v_new)
        g_last = gi[..., -1]
        k_dec = ki * jnp.exp(g_last[..., None] - gi)[..., None]
        state = state * jnp.exp(g_last)[..., None, None] + jnp.einsum("bhck,bhcv->bhkv", k_dec, v_new)
        return state, o

    state0 = jnp.zeros((b_, h_, dk, dv), jnp.float32)
    _, o = lax.scan(step, state0, (qc, kc, u, w, gcum, attn_intra))
    return o.transpose(1, 0, 3, 2, 4).reshape(b_, s_, h_, dv)


def gated_deltanet(h, w_in, conv_w, a_log, dt_bias, norm_g, w_out):
    b_, s_, _ = h.shape
    proj = h @ w_in
    qkv = proj[..., :3 * GDN_WIDTH]
    z = proj[..., 3 * GDN_WIDTH:4 * GDN_WIDTH]
    b_logit = proj[..., 4 * GDN_WIDTH:4 * GDN_WIDTH + GDN_HEADS]
    a_logit = proj[..., 4 * GDN_WIDTH + GDN_HEADS:]
    qkv = jax.nn.silu(causal_depthwise_conv(qkv, conv_w))
    q, k, v = [t.reshape(b_, s_, GDN_HEADS, GDN_HEAD_DIM) for t in jnp.split(qkv, 3, axis=-1)]
    q, k = l2_norm(q), l2_norm(k)
    beta = jax.nn.sigmoid(b_logit.astype(jnp.float32))
    g = -jnp.exp(a_log.astype(jnp.float32)) * jax.nn.softplus(
        a_logit.astype(jnp.float32) + dt_bias.astype(jnp.float32))
    o = gated_delta_rule(q, k, v, g, beta)
    zf = z.reshape(b_, s_, GDN_HEADS, GDN_HEAD_DIM).astype(jnp.float32)
    o = rms_norm(o, norm_g) * jax.nn.silu(zf)
    return o.reshape(b_, s_, GDN_WIDTH).astype(h.dtype) @ w_out


def mla_shared_kv(x, c, ada_w, ada_b, norm_g, w_dkv, kv_norm_g, w_ukv, k_norm_g, positions):
    b_, s_, _ = x.shape
    shift, scale = jnp.split(jax.nn.silu(c) @ ada_w + ada_b, 2, axis=-1)
    h = modulate(x, norm_g, shift, scale)
    ckv = h @ w_dkv
    latent = rms_norm(ckv[..., :KV_LORA], kv_norm_g)
    k_rope = ckv[..., KV_LORA:]
    kv = (latent @ w_ukv).reshape(b_, s_, MLA_HEADS, QK_NOPE + V_HEAD)
    k_nope, v = kv[..., :QK_NOPE], kv[..., QK_NOPE:]
    k = jnp.concatenate(
        [k_nope, jnp.broadcast_to(k_rope[:, :, None, :], (b_, s_, MLA_HEADS, QK_ROPE))], axis=-1)
    k = rms_norm(k, k_norm_g)
    k = jnp.concatenate([k[..., :QK_NOPE], rotary(k[..., QK_NOPE:], positions)], axis=-1)
    return k, v


def block_causal_attention(q, k, v):
    b_, s_, h_, dq = q.shape
    nqb = s_ // QBLOCK
    qb = q.reshape(b_, nqb, QBLOCK, h_, dq).transpose(1, 0, 2, 3, 4)
    key_chunk = jnp.arange(s_) // CHUNK
    scale = QK_HEAD ** -0.5

    def one_block(args):
        qi, blk = args
        s = jnp.einsum("bqhd,bkhd->bhqk", qi, k).astype(jnp.float32) * scale
        q_chunk = (blk * QBLOCK + jnp.arange(QBLOCK)) // CHUNK
        mask = key_chunk[None, :] <= q_chunk[:, None]
        s = jnp.where(mask, s, jnp.finfo(jnp.float32).min)
        p = jax.nn.softmax(s, axis=-1)
        return jnp.einsum("bhqk,bkhd->bqhd", p.astype(v.dtype), v)

    o = lax.map(one_block, (qb, jnp.arange(nqb)))
    return o.transpose(1, 0, 2, 3, 4).reshape(b_, s_, h_ * v.shape[-1])


def mla_attention(h, k, v, positions, w_dq, q_lora_norm_g, w_uq, q_norm_g, w_out):
    b_, s_, _ = h.shape
    q = rms_norm(h @ w_dq, q_lora_norm_g) @ w_uq
    q = rms_norm(q.reshape(b_, s_, MLA_HEADS, QK_HEAD), q_norm_g)
    q = jnp.concatenate([q[..., :QK_NOPE], rotary(q[..., QK_NOPE:], positions)], axis=-1)
    return block_causal_attention(q, k, v) @ w_out


def setup_inputs(seed: int = 0) -> dict:
    key = jax.random.key(seed)
    ks = iter(jax.random.split(key, 40))
    f32 = jnp.float32

    def nrm(shape, fan_in, scale=1.0):
        return jax.random.normal(next(ks), shape, f32) * (scale * fan_in ** -0.5)

    def gain(shape):
        return 1.0 + 0.02 * jax.random.normal(next(ks), shape, f32)

    def bias(shape):
        return 0.02 * jax.random.normal(next(ks), shape, f32)

    x = jax.random.normal(next(ks), (BATCH, SEQ, D_MODEL), f32)
    c = jax.random.normal(next(ks), (BATCH, D_MODEL), f32)
    offs = jax.random.randint(next(ks), (BATCH, 1), 0, MAX_POS_OFFSET)
    positions = (offs + jnp.arange(SEQ)[None, :]).astype(jnp.int32)

    dt = jnp.exp(jax.random.uniform(next(ks), (N_A_LAYERS, GDN_HEADS), f32,
                                    np.log(1e-3), np.log(1e-1)))
    return {
        "x": x,
        "c": c,
        "positions": positions,
        "ada_w": nrm((DEPTH, D_MODEL, N_MOD * D_MODEL), D_MODEL, 0.5),
        "ada_b": bias((DEPTH, N_MOD * D_MODEL)),
        "norm_g": gain((DEPTH, 3, D_MODEL)),
        "ffn_w_in": nrm((DEPTH, 2, D_MODEL, 2 * D_FF), D_MODEL),
        "ffn_w_out": nrm((DEPTH, 2, D_FF, D_MODEL), D_FF),
        "gdn_w_in": nrm((N_A_LAYERS, D_MODEL, GDN_PROJ), D_MODEL),
        "gdn_conv_w": nrm((N_A_LAYERS, CONV_K, 3 * GDN_WIDTH), CONV_K),
        "gdn_a_log": jnp.log(jax.random.uniform(next(ks), (N_A_LAYERS, GDN_HEADS), f32, 1.0, 16.0)),
        "gdn_dt_bias": dt + jnp.log(-jnp.expm1(-dt)),
        "gdn_norm_g": gain((N_A_LAYERS, GDN_HEAD_DIM)),
        "gdn_w_out": nrm((N_A_LAYERS, GDN_WIDTH, D_MODEL), GDN_WIDTH),
        "kv_ada_w": nrm((D_MODEL, 2 * D_MODEL), D_MODEL, 0.5),
        "kv_ada_b": bias((2 * D_MODEL,)),
        "kv_norm_g": gain((D_MODEL,)),
        "mla_w_dkv": nrm((D_MODEL, KV_LORA + QK_ROPE), D_MODEL),
        "mla_kv_norm_g": gain((KV_LORA,)),
        "mla_w_ukv": nrm((KV_LORA, MLA_HEADS * (QK_NOPE + V_HEAD)), KV_LORA),
        "mla_k_norm_g": gain((QK_HEAD,)),
        "mla_w_dq": nrm((N_B_LAYERS, D_MODEL, Q_LORA), D_MODEL),
        "mla_q_lora_norm_g": gain((N_B_LAYERS, Q_LORA)),
        "mla_w_uq": nrm((N_B_LAYERS, Q_LORA, MLA_HEADS * QK_HEAD), Q_LORA),
        "mla_q_norm_g": gain((N_B_LAYERS, QK_HEAD)),
        "mla_w_out": nrm((N_B_LAYERS, MLA_HEADS * V_HEAD, D_MODEL), MLA_HEADS * V_HEAD),
    }


def reference(x, c, positions, ada_w, ada_b, norm_g, ffn_w_in, ffn_w_out,
              gdn_w_in, gdn_conv_w, gdn_a_log, gdn_dt_bias, gdn_norm_g, gdn_w_out,
              kv_ada_w, kv_ada_b, kv_norm_g, mla_w_dkv, mla_kv_norm_g, mla_w_ukv, mla_k_norm_g,
              mla_w_dq, mla_q_lora_norm_g, mla_w_uq, mla_q_norm_g, mla_w_out):
    b_ = x.shape[0]
    c_act = jax.nn.silu(c)
    shared_k = shared_v = None
    for l in range(DEPTH):
        mod = (c_act @ ada_w[l] + ada_b[l]).reshape(b_, N_MOD, D_MODEL)
        h = modulate(x, norm_g[l, 0], mod[:, 0], mod[:, 1])
        x = x + 0.5 * mod[:, 2][:, None, :] * swiglu(h, ffn_w_in[l, 0], ffn_w_out[l, 0])
        h = modulate(x, norm_g[l, 1], mod[:, 3], mod[:, 4])
        if l < N_A_LAYERS:
            y = gated_deltanet(h, gdn_w_in[l], gdn_conv_w[l], gdn_a_log[l], gdn_dt_bias[l],
                               gdn_norm_g[l], gdn_w_out[l])
        else:
            j = l - N_A_LAYERS
            y = mla_attention(h, shared_k, shared_v, positions, mla_w_dq[j], mla_q_lora_norm_g[j],
                              mla_w_uq[j], mla_q_norm_g[j], mla_w_out[j])
        x = x + mod[:, 5][:, None, :] * y
        h = modulate(x, norm_g[l, 2], mod[:, 6], mod[:, 7])
        x = x + 0.5 * mod[:, 8][:, None, :] * swiglu(h, ffn_w_in[l, 1], ffn_w_out[l, 1])
        if l == N_A_LAYERS - 1:
            shared_k, shared_v = mla_shared_kv(x, c, kv_ada_w, kv_ada_b, kv_norm_g, mla_w_dkv,
                                               mla_kv_norm_g, mla_w_ukv, mla_k_norm_g, positions)
    return x
```

```python
import functools

import jax
import jax.numpy as jnp
from jax import lax
from jax.experimental import pallas as pl
from jax.experimental.pallas import tpu as pltpu

F32 = jnp.float32
BF16 = jnp.bfloat16

EPS = 1e-6
CHUNK = 64
ROPE_BASE = 10000.0
LANE = 128
VMEM_LIMIT = 56 * 1024 * 1024

GDN_HEAD_DIM = 128
QK_NOPE = 128
QK_ROPE = 64
V_HEAD = 128
QK_PAD = 256


def _cparams(*sem):
    return pltpu.CompilerParams(dimension_semantics=sem, vmem_limit_bytes=VMEM_LIMIT)


def _silu(t):
    return t / (1.0 + jnp.exp(-t))


def _dot(a, b):
    return jnp.dot(a, b, preferred_element_type=F32)


def _dot_nt(a, b):
    return lax.dot_general(a, b, (((1,), (1,)), ((), ())), preferred_element_type=F32)


def _dot_tn(a, b):
    return lax.dot_general(a, b, (((0,), (0,)), ((), ())), preferred_element_type=F32)


def _split3(t):
    hi = t.astype(BF16)
    r1 = t - hi.astype(F32)
    mid = r1.astype(BF16)
    lo = (r1 - mid.astype(F32)).astype(BF16)
    return hi, mid, lo


def _const_spec(shape):
    nd = len(shape)
    return pl.BlockSpec(shape, lambda *_: (0,) * nd, pipeline_mode=pl.Buffered(1))


def _modulated(x, g, shift, scale):
    ms = jnp.mean(x * x, axis=-1, keepdims=True)
    return (x * lax.rsqrt(ms + EPS) * g) * (1.0 + scale) + shift


def _mod_kernel(c_ref, w_ref, b_ref, o_ref):
    ca = _silu(c_ref[...])
    c_hi = ca.astype(BF16)
    c_lo = (ca - c_hi.astype(F32)).astype(BF16)
    w = w_ref[0]
    w_hi = w.astype(BF16)
    w_lo = (w - w_hi.astype(F32)).astype(BF16)
    o_ref[0] = _dot(c_hi, w_hi) + (_dot(c_lo, w_hi) + _dot(c_hi, w_lo)) + b_ref[0]


def _modulation(c_pad, w, b):
    nl, d, n = w.shape
    tn = 1024
    return pl.pallas_call(
        _mod_kernel,
        grid=(nl, n // tn),
        in_specs=[
            pl.BlockSpec((8, d), lambda l, j: (0, 0)),
            pl.BlockSpec((1, d, tn), lambda l, j: (l, 0, j)),
            pl.BlockSpec((1, 1, tn), lambda l, j: (l, 0, j)),
        ],
        out_specs=pl.BlockSpec((1, 8, tn), lambda l, j: (l, 0, j)),
        out_shape=jax.ShapeDtypeStruct((nl, 8, n), F32),
        compiler_params=_cparams("parallel", "parallel"),
        name="adaln_mod",
    )(c_pad, w, b.reshape(nl, 1, n))


def _rope_kernel(pos_ref, cos_ref, sin_ref):
    half = QK_ROPE // 2
    pos = pos_ref[0].astype(F32)
    lane = lax.broadcasted_iota(jnp.int32, (1, LANE), 1)
    idx = jnp.where(lane < half, lane, lane - half).astype(F32)
    inv_freq = jnp.exp(idx * (-jnp.log(ROPE_BASE) / half))
    ang = pos * inv_freq
    valid = lane < QK_ROPE
    cos_ref[0] = jnp.where(valid, jnp.cos(ang), 0.0)
    sin_ref[0] = jnp.where(valid, jnp.where(lane < half, -jnp.sin(ang), jnp.sin(ang)), 0.0)


def _rope_tables(positions):
    b, s = positions.shape
    ts = 512
    out = jax.ShapeDtypeStruct((b, s, LANE), F32)
    return pl.pallas_call(
        _rope_kernel,
        grid=(b, s // ts),
        in_specs=[pl.BlockSpec((1, ts, 1), lambda i, j: (i, j, 0))],
        out_specs=[pl.BlockSpec((1, ts, LANE), lambda i, j: (i, j, 0))] * 2,
        out_shape=[out, out],
        compiler_params=_cparams("parallel", "parallel"),
        name="rope_tables",
    )(positions.reshape(b, s, 1))


def _rotate(t, cos, sin):
    half = QK_ROPE // 2
    lane = lax.broadcasted_iota(jnp.int32, t.shape, 1)
    swapped = jnp.where(lane < half, pltpu.roll(t, LANE - half, 1), pltpu.roll(t, half, 1))
    return t * cos + swapped * sin


def _ffn_kernel(*refs, pre):
    if pre:
        (x_ref, y_ref, wo_ref, gmix_ref, g_ref, sh_ref, sc_ref, gate_ref,
         wg_ref, wu_ref, wd_ref, out_ref) = refs
    else:
        x_ref, g_ref, sh_ref, sc_ref, gate_ref, wg_ref, wu_ref, wd_ref, out_ref = refs
    x = x_ref[0]
    if pre:
        x = x + gmix_ref[0] * _dot(y_ref[0], wo_ref[...])
    hb = _modulated(x, g_ref[...], sh_ref[0], sc_ref[0]).astype(BF16)
    gate = _dot(hb, wg_ref[...])
    up = _dot(hb, wu_ref[...])
    act = (_silu(gate) * up).astype(BF16)
    out_ref[0] = x + (0.5 * gate_ref[0]) * _dot(act, wd_ref[...])


def _ffn(x, norm_g, shift, scale, gate, w_gate, w_up, w_down, pre=None, tm=512):
    b, s, d = x.shape
    f = w_gate.shape[1]
    tile = pl.BlockSpec((1, tm, d), lambda i, j: (i, j, 0))
    vec = pl.BlockSpec((1, 1, d), lambda i, j: (i, 0, 0))
    args, specs = [x], [tile]
    if pre is not None:
        y, w_o, g_mix = pre
        args += [y, w_o, g_mix]
        specs += [pl.BlockSpec((1, tm, y.shape[-1]), lambda i, j: (i, j, 0)), _const_spec(w_o.shape), vec]
    args += [norm_g.reshape(1, d), shift, scale, gate, w_gate, w_up, w_down]
    specs += [_const_spec((1, d)), vec, vec, vec,
              _const_spec((d, f)), _const_spec((d, f)), _const_spec((f, d))]
    return pl.pallas_call(
        functools.partial(_ffn_kernel, pre=pre is not None),
        grid=(b, s // tm),
        in_specs=specs,
        out_specs=tile,
        out_shape=jax.ShapeDtypeStruct((b, s, d), F32),
        compiler_params=_cparams("parallel", "parallel"),
        name="ffn_pre" if pre is not None else "ffn",
    )(*args)


def _gdn_in_kernel(x_ref, g_ref, sh_ref, sc_ref, w_ref, wba_ref, qkv_ref, z_ref, ba_ref):
    hb = _modulated(x_ref[0], g_ref[...], sh_ref[0], sc_ref[0]).astype(BF16)
    width = z_ref.shape[-1]
    p = _dot(hb, w_ref[...])
    qkv_ref[0] = p[:, :3 * width]
    z_ref[0] = p[:, 3 * width:]
    ba_ref[0] = _dot(hb, wba_ref[...])


def _gdn_in(x, norm_g, shift, scale, w_qkvz, w_ba, tm=512):
    b, s, d = x.shape
    width = w_qkvz.shape[1] // 4
    tile = lambda n: pl.BlockSpec((1, tm, n), lambda i, j: (i, j, 0))
    vec = pl.BlockSpec((1, 1, d), lambda i, j: (i, 0, 0))
    return pl.pallas_call(
        _gdn_in_kernel,
        grid=(b, s // tm),
        in_specs=[tile(d), _const_spec((1, d)), vec, vec, _const_spec(w_qkvz.shape), _const_spec(w_ba.shape)],
        out_specs=[tile(3 * width), tile(width), tile(2 * LANE)],
        out_shape=[jax.ShapeDtypeStruct((b, s, 3 * width), F32),
                   jax.ShapeDtypeStruct((b, s, width), F32),
                   jax.ShapeDtypeStruct((b, s, 2 * LANE), F32)],
        compiler_params=_cparams("parallel", "parallel"),
        name="gdn_in",
    )(x, norm_g.reshape(1, d), shift, scale, w_qkvz, w_ba)


def _cumsum_dot(tri, t):
    hi, mid, lo = _split3(t)
    return _dot(tri, hi) + (_dot(tri, mid) + _dot(tri, lo))


def _gdn_core_kernel(qkv_ref, z_ref, ba_ref, convw_ref, alog_ref, dtb_ref, ng_ref, o_ref,
                     state_ref, ext_ref, *, heads):
    c = CHUNK
    dk = GDN_HEAD_DIM
    width = heads * dk

    @pl.when(pl.program_id(1) == 0)
    def _():
        state_ref[...] = jnp.zeros_like(state_ref)
        ext_ref[0:8, :] = jnp.zeros((8, 3 * width), F32)

    cur = qkv_ref[0]
    ext_ref[8:8 + c, :] = cur
    k_taps = convw_ref.shape[0]
    acc = ext_ref[8:8 + c, :] * convw_ref[k_taps - 1:k_taps, :]
    for j in range(k_taps - 1):
        off = 8 - (k_taps - 1) + j
        acc = acc + ext_ref[off:off + c, :] * convw_ref[j:j + 1, :]
    ext_ref[0:8, :] = cur[c - 8:c, :]
    act = _silu(acc)

    ba = ba_ref[0]
    beta = 1.0 / (1.0 + jnp.exp(-ba[:, :LANE]))
    al = ba[:, LANE:] + dtb_ref[...]
    softplus = jnp.maximum(al, 0.0) + jnp.log1p(jnp.exp(-jnp.abs(al)))
    lane = lax.broadcasted_iota(jnp.int32, (c, LANE), 1)
    gl = jnp.where(lane < heads, -jnp.exp(alog_ref[...]) * softplus, 0.0)

    row = lax.broadcasted_iota(jnp.int32, (c, c), 0)
    col = lax.broadcasted_iota(jnp.int32, (c, c), 1)
    incl = row >= col
    strict = row > col
    gcum = _cumsum_dot(incl.astype(BF16), gl)
    r2 = lax.broadcasted_iota(jnp.int32, (LANE, LANE), 0)
    c2 = lax.broadcasted_iota(jnp.int32, (LANE, LANE), 1)
    upper = ((r2 <= c2) & (c2 < c)).astype(BF16)
    gl_t = jnp.transpose(jnp.concatenate([gl, jnp.zeros((LANE - c, LANE), F32)], axis=0))
    gcum_t = _cumsum_dot_rows(gl_t[:8, :], upper)

    eye = (row == col).astype(F32)
    for h in range(heads):
        sl = slice(h * dk, (h + 1) * dk)
        q = act[:, h * dk:(h + 1) * dk]
        k = act[:, width + h * dk:width + (h + 1) * dk]
        v = act[:, 2 * width + h * dk:2 * width + (h + 1) * dk]
        q = q * lax.rsqrt(jnp.sum(q * q, axis=-1, keepdims=True) + EPS) * (dk ** -0.5)
        k = k * lax.rsqrt(jnp.sum(k * k, axis=-1, keepdims=True) + EPS)
        gc = gcum[:, h:h + 1]
        gr = gcum_t[h:h + 1, :c]
        bcol = beta[:, h:h + 1]
        decay = jnp.where(incl, jnp.exp(jnp.where(incl, gc - gr, 0.0)), 0.0)
        eg = jnp.exp(gc)
        kb = k * bcol
        k16 = k.astype(BF16)
        a = jnp.where(strict, _dot_nt(kb.astype(BF16), k16) * decay, 0.0)
        t_inv = eye - a
        pw = a
        for _ in range(c.bit_length() - 2):
            pw16 = pw.astype(BF16)
            pw = _dot(pw16, pw16)
            t_inv = t_inv + _dot(t_inv.astype(BF16), pw.astype(BF16))
        rhs = jnp.concatenate([v * bcol, kb * eg], axis=1).astype(BF16)
        sol = _dot(t_inv.astype(BF16), rhs)
        u, w = sol[:, :dk], sol[:, dk:]
        attn = jnp.where(incl, _dot_nt(q.astype(BF16), k16) * decay, 0.0)
        st = state_ref[h]
        st16 = st.astype(BF16)
        v_new = u - _dot(w.astype(BF16), st16)
        vn16 = v_new.astype(BF16)
        o = _dot((q * eg).astype(BF16), st16) + _dot(attn.astype(BF16), vn16)
        g_last = gc[c - 1:c, :]
        k_dec = (k * jnp.exp(g_last - gc)).astype(BF16)
        state_ref[h] = st * jnp.exp(g_last) + _dot_tn(k_dec, vn16)
        o = o * lax.rsqrt(jnp.mean(o * o, axis=-1, keepdims=True) + EPS) * ng_ref[...]
        o_ref[0, :, sl] = (o * _silu(z_ref[0, :, sl])).astype(o_ref.dtype)


def _cumsum_dot_rows(t, tri):
    hi, mid, lo = _split3(t)
    return _dot(hi, tri) + (_dot(mid, tri) + _dot(lo, tri))


def _gdn_core(qkv, z, ba, conv_w, a_log, dt_bias, norm_g):
    b, s, width = z.shape
    heads = width // GDN_HEAD_DIM
    pad = lambda t: jnp.pad(t.reshape(1, heads), ((0, 0), (0, LANE - heads)))
    return pl.pallas_call(
        functools.partial(_gdn_core_kernel, heads=heads),
        grid=(b, s // CHUNK),
        in_specs=[
            pl.BlockSpec((1, CHUNK, 3 * width), lambda i, j: (i, j, 0)),
            pl.BlockSpec((1, CHUNK, width), lambda i, j: (i, j, 0)),
            pl.BlockSpec((1, CHUNK, 2 * LANE), lambda i, j: (i, j, 0)),
            _const_spec(conv_w.shape), _const_spec((1, LANE)), _const_spec((1, LANE)),
            _const_spec((1, GDN_HEAD_DIM)),
        ],
        out_specs=pl.BlockSpec((1, CHUNK, width), lambda i, j: (i, j, 0)),
        out_shape=jax.ShapeDtypeStruct((b, s, width), BF16),
        scratch_shapes=[pltpu.VMEM((heads, GDN_HEAD_DIM, GDN_HEAD_DIM), F32),
                        pltpu.VMEM((8 + CHUNK, 3 * width), F32)],
        compiler_params=_cparams("parallel", "arbitrary"),
        name="gdn_core",
    )(qkv, z, ba, conv_w, pad(a_log), pad(dt_bias), norm_g.reshape(1, GDN_HEAD_DIM))


def _kv_kernel(x_ref, g_ref, sh_ref, sc_ref, wdl_ref, wdr_ref, lg_ref, wk_ref, wv_ref,
               kgn_ref, kgr_ref, cos_ref, sin_ref, k_ref, vt_ref, *, heads):
    hb = _modulated(x_ref[0], g_ref[...], sh_ref[0], sc_ref[0]).astype(BF16)
    lat = _dot(hb, wdl_ref[...])
    rope = _dot(hb, wdr_ref[...])
    lat = lat * lax.rsqrt(jnp.mean(lat * lat, axis=-1, keepdims=True) + EPS) * lg_ref[...]
    lat16 = lat.astype(BF16)
    k_nope = _dot(lat16, wk_ref[...])
    v = _dot(lat16, wv_ref[...])
    rope_sq = jnp.sum(rope * rope, axis=-1, keepdims=True)
    for h in range(heads):
        sl = slice(h * QK_NOPE, (h + 1) * QK_NOPE)
        kn = k_nope[:, sl]
        inv = lax.rsqrt((jnp.sum(kn * kn, axis=-1, keepdims=True) + rope_sq) / (QK_NOPE + QK_ROPE) + EPS)
        k_ref[0, h, :, :QK_NOPE] = (kn * inv * kgn_ref[...]).astype(k_ref.dtype)
        kr = _rotate(rope * inv * kgr_ref[...], cos_ref[0], sin_ref[0])
        k_ref[0, h, :, QK_NOPE:] = kr.astype(k_ref.dtype)
        vt_ref[0, h, 0] = jnp.transpose(v[:, h * V_HEAD:(h + 1) * V_HEAD]).astype(vt_ref.dtype)


def _shared_kv(x, norm_g, shift, scale, w_dl, w_dr, lat_g, w_k, w_v, kg_nope, kg_rope, cos, sin, tm=512):
    b, s, d = x.shape
    heads = w_k.shape[1] // QK_NOPE
    tile = lambda n: pl.BlockSpec((1, tm, n), lambda i, j: (i, j, 0))
    vec = pl.BlockSpec((1, 1, d), lambda i, j: (i, 0, 0))
    return pl.pallas_call(
        functools.partial(_kv_kernel, heads=heads),
        grid=(b, s // tm),
        in_specs=[tile(d), _const_spec((1, d)), vec, vec, _const_spec(w_dl.shape), _const_spec(w_dr.shape),
                  _const_spec(lat_g.shape), _const_spec(w_k.shape), _const_spec(w_v.shape),
                  _const_spec(kg_nope.shape), _const_spec(kg_rope.shape), tile(LANE), tile(LANE)],
        out_specs=[pl.BlockSpec((1, heads, tm, QK_PAD), lambda i, j: (i, 0, j, 0)),
                   pl.BlockSpec((1, heads, 1, V_HEAD, tm), lambda i, j: (i, 0, j, 0, 0))],
        out_shape=[jax.ShapeDtypeStruct((b, heads, s, QK_PAD), BF16),
                   jax.ShapeDtypeStruct((b, heads, s // tm, V_HEAD, tm), BF16)],
        compiler_params=_cparams("parallel", "parallel"),
        name="mla_kv",
    )(x, norm_g.reshape(1, d), shift, scale, w_dl, w_dr, lat_g, w_k, w_v, kg_nope, kg_rope, cos, sin)


def _q_kernel(x_ref, g_ref, sh_ref, sc_ref, wdq_ref, qlg_ref, wqn_ref, wqr_ref,
              qgn_ref, qgr_ref, cos_ref, sin_ref, q_ref, *, heads):
    hb = _modulated(x_ref[0], g_ref[...], sh_ref[0], sc_ref[0]).astype(BF16)
    ql = _dot(hb, wdq_ref[...])
    ql = ql * lax.rsqrt(jnp.mean(ql * ql, axis=-1, keepdims=True) + EPS) * qlg_ref[...]
    ql16 = ql.astype(BF16)
    q_nope = _dot(ql16, wqn_ref[...])
    q_rope = _dot(ql16, wqr_ref[...])
    sm_scale = (QK_NOPE + QK_ROPE) ** -0.5
    for h in range(heads):
        sl = slice(h * LANE, (h + 1) * LANE)
        qn, qr = q_nope[:, sl], q_rope[:, sl]
        ssq = jnp.sum(qn * qn, axis=-1, keepdims=True) + jnp.sum(qr * qr, axis=-1, keepdims=True)
        inv = lax.rsqrt(ssq / (QK_NOPE + QK_ROPE) + EPS) * sm_scale
        q_ref[0, h, :, :QK_NOPE] = (qn * inv * qgn_ref[...]).astype(q_ref.dtype)
        q_ref[0, h, :, QK_NOPE:] = _rotate(qr * inv * qgr_ref[...], cos_ref[0], sin_ref[0]).astype(q_ref.dtype)


def _mla_q(x, norm_g, shift, scale, w_dq, ql_g, w_qn, w_qr, qg_nope, qg_rope, cos, sin, tm=512):
    b, s, d = x.shape
    heads = w_qn.shape[1] // QK_NOPE
    tile = lambda n: pl.BlockSpec((1, tm, n), lambda i, j: (i, j, 0))
    vec = pl.BlockSpec((1, 1, d), lambda i, j: (i, 0, 0))
    return pl.pallas_call(
        functools.partial(_q_kernel, heads=heads),
        grid=(b, s // tm),
        in_specs=[tile(d), _const_spec((1, d)), vec, vec, _const_spec(w_dq.shape), _const_spec(ql_g.shape),
                  _const_spec(w_qn.shape), _const_spec(w_qr.shape), _const_spec(qg_nope.shape),
                  _const_spec(qg_rope.shape), tile(LANE), tile(LANE)],
        out_specs=pl.BlockSpec((1, heads, tm, QK_PAD), lambda i, j: (i, 0, j, 0)),
        out_shape=jax.ShapeDtypeStruct((b, heads, s, QK_PAD), BF16),
        compiler_params=_cparams("parallel", "parallel"),
        name="mla_q",
    )(x, norm_g.reshape(1, d), shift, scale, w_dq, ql_g, w_qn, w_qr, qg_nope, qg_rope, cos, sin)


def _attn_kernel(q_ref, k_ref, vt_ref, o_ref, *, tq):
    qi = pl.program_id(2)
    q = q_ref[0, 0]

    def block(j, carry, masked):
        m, l, acc = carry
        kj = k_ref[0, 0, pl.ds(pl.multiple_of(j * tq, tq), tq), :]
        s = _dot_nt(kj, q)
        if masked:
            key_c = lax.broadcasted_iota(jnp.int32, (tq, tq), 0) // CHUNK
            qry_c = lax.broadcasted_iota(jnp.int32, (tq, tq), 1) // CHUNK
            s = jnp.where(key_c <= qry_c, s, -jnp.inf)
        m_new = jnp.maximum(m, jnp.max(s, axis=0, keepdims=True))
        alpha = jnp.exp(m - m_new)
        p = jnp.exp(s - m_new)
        l = alpha * l + jnp.sum(p, axis=0, keepdims=True)
        acc = alpha * acc + _dot(vt_ref[0, 0, j], p.astype(BF16))
        return m_new, l, acc

    init = (jnp.full((1, tq), -jnp.inf, F32), jnp.zeros((1, tq), F32), jnp.zeros((V_HEAD, tq), F32))
    carry = block(qi, init, True)
    m, l, acc = lax.fori_loop(0, qi, lambda j, cr: block(j, cr, False), carry)
    o_ref[0] = jnp.transpose(acc / l).astype(o_ref.dtype)


def _attention(q, k, vt):
    b, heads, s, dq = q.shape
    tq = vt.shape[-1]
    nkb = s // tq
    return pl.pallas_call(
        functools.partial(_attn_kernel, tq=tq),
        grid=(b, heads, s // tq),
        in_specs=[pl.BlockSpec((1, 1, tq, dq), lambda i, h, j: (i, h, j, 0)),
                  pl.BlockSpec((1, 1, s, dq), lambda i, h, j: (i, h, 0, 0)),
                  pl.BlockSpec((1, 1, nkb, V_HEAD, tq), lambda i, h, j: (i, h, 0, 0, 0))],
        out_specs=pl.BlockSpec((1, tq, V_HEAD), lambda i, h, j: (i, j, h)),
        out_shape=jax.ShapeDtypeStruct((b, s, heads * V_HEAD), BF16),
        compiler_params=_cparams("parallel", "parallel", "parallel"),
        name="mla_attn",
    )(q, k, vt)


def _pad_cols(w, n):
    return jnp.pad(w, ((0, 0), (0, n - w.shape[1])))


def kernel(x, c, positions, ada_w, ada_b, norm_g, ffn_w_in, ffn_w_out, gdn_w_in, gdn_conv_w, gdn_a_log,
           gdn_dt_bias, gdn_norm_g, gdn_w_out, kv_ada_w, kv_ada_b, kv_norm_g, mla_w_dkv, mla_kv_norm_g,
           mla_w_ukv, mla_k_norm_g, mla_w_dq, mla_q_lora_norm_g, mla_w_uq, mla_q_norm_g, mla_w_out):
    b, s, d = x.shape
    depth = ada_w.shape[0]
    n_a = gdn_w_in.shape[0]
    n_mod = ada_w.shape[2] // d
    d_ff = ffn_w_out.shape[2]
    width = gdn_w_out.shape[1]
    g_heads = width // GDN_HEAD_DIM
    kv_lora = mla_kv_norm_g.shape[0]
    m_heads = mla_w_ukv.shape[1] // (QK_NOPE + V_HEAD)

    c_pad = jnp.pad(c, ((0, 8 - b), (0, 0)))
    mod = _modulation(c_pad, ada_w, ada_b)[:, :b].reshape(depth, b, n_mod, 1, d)
    kv_mod = _modulation(c_pad, kv_ada_w[None], kv_ada_b[None])[0, :b].reshape(b, 2, 1, d)
    cos, sin = _rope_tables(positions)

    k_sh = vt_sh = None
    for l in range(depth):
        m = lambda i: mod[l, :, i]
        w_in = ffn_w_in[l].astype(BF16)
        w_out = ffn_w_out[l].astype(BF16)
        x = _ffn(x, norm_g[l, 0], m(0), m(1), m(2), w_in[0, :, :d_ff], w_in[0, :, d_ff:], w_out[0])
        if l < n_a:
            w = gdn_w_in[l]
            w_ba = jnp.concatenate([_pad_cols(w[:, 4 * width:4 * width + g_heads], LANE),
                                    _pad_cols(w[:, 4 * width + g_heads:], LANE)], axis=1).astype(BF16)
            qkv, z, ba = _gdn_in(x, norm_g[l, 1], m(3), m(4), w[:, :4 * width].astype(BF16), w_ba)
            y = _gdn_core(qkv, z, ba, gdn_conv_w[l], gdn_a_log[l], gdn_dt_bias[l], gdn_norm_g[l])
            w_o = gdn_w_out[l].astype(BF16)
        else:
            j = l - n_a
            w_uq = mla_w_uq[j].reshape(-1, m_heads, QK_NOPE + QK_ROPE)
            w_qn = w_uq[:, :, :QK_NOPE].reshape(-1, m_heads * QK_NOPE).astype(BF16)
            w_qr = jnp.pad(w_uq[:, :, QK_NOPE:], ((0, 0), (0, 0), (0, LANE - QK_ROPE)))
            w_qr = w_qr.reshape(-1, m_heads * LANE).astype(BF16)
            qg = mla_q_norm_g[j]
            q = _mla_q(x, norm_g[l, 1], m(3), m(4), mla_w_dq[j].astype(BF16),
                       mla_q_lora_norm_g[j].reshape(1, -1), w_qn, w_qr,
                       qg[:QK_NOPE].reshape(1, -1), _pad_cols(qg[QK_NOPE:].reshape(1, -1), LANE), cos, sin)
            y = _attention(q, k_sh, vt_sh)
            w_o = mla_w_out[j].astype(BF16)
        x = _ffn(x, norm_g[l, 2], m(6), m(7), m(8), w_in[1, :, :d_ff], w_in[1, :, d_ff:], w_out[1],
                 pre=(y, w_o, m(5)))
        if l == n_a - 1:
            w_ukv = mla_w_ukv.reshape(kv_lora, m_heads, QK_NOPE + V_HEAD)
            w_k = w_ukv[:, :, :QK_NOPE].reshape(kv_lora, m_heads * QK_NOPE).astype(BF16)
            w_v = w_ukv[:, :, QK_NOPE:].reshape(kv_lora, m_heads * V_HEAD).astype(BF16)
            kg = mla_k_norm_g
            k_sh, vt_sh = _shared_kv(
                x, kv_norm_g, kv_mod[:, 0], kv_mod[:, 1], mla_w_dkv[:, :kv_lora].astype(BF16),
                _pad_cols(mla_w_dkv[:, kv_lora:], LANE).astype(BF16), mla_kv_norm_g.reshape(1, -1), w_k, w_v,
                kg[:QK_NOPE].reshape(1, -1), _pad_cols(kg[QK_NOPE:].reshape(1, -1), LANE), cos, sin)
    return x
```

```python
import functools

import jax
import jax.numpy as jnp
from jax import lax
from jax.experimental import pallas as pl
from jax.experimental.pallas import tpu as pltpu

F32 = jnp.float32
BF16 = jnp.bfloat16

EPS = 1e-6
CHUNK = 64
ROPE_BASE = 10000.0
LANE = 128
VMEM_LIMIT = 56 * 1024 * 1024

GDN_HEAD_DIM = 128
QK_NOPE = 128
QK_ROPE = 64
V_HEAD = 128
QK_PAD = 256


def _cparams(*sem):
    return pltpu.CompilerParams(dimension_semantics=sem, vmem_limit_bytes=VMEM_LIMIT)


def _silu(t):
    return t / (1.0 + jnp.exp(-t))


def _dot(a, b):
    return jnp.dot(a, b, preferred_element_type=F32)


def _dot_nt(a, b):
    return lax.dot_general(a, b, (((1,), (1,)), ((), ())), preferred_element_type=F32)


def _dot_tn(a, b):
    return lax.dot_general(a, b, (((0,), (0,)), ((), ())), preferred_element_type=F32)


def _split3(t):
    hi = t.astype(BF16)
    r1 = t - hi.astype(F32)
    mid = r1.astype(BF16)
    lo = (r1 - mid.astype(F32)).astype(BF16)
    return hi, mid, lo


def _const_spec(shape):
    nd = len(shape)
    return pl.BlockSpec(shape, lambda *_: (0,) * nd, pipeline_mode=pl.Buffered(1))


def _modulated(x, g, shift, scale):
    ms = jnp.mean(x * x, axis=-1, keepdims=True)
    return (x * lax.rsqrt(ms + EPS) * g) * (1.0 + scale) + shift


def _mod_kernel(c_ref, w_ref, b_ref, o_ref):
    ca = _silu(c_ref[...])
    c_hi = ca.astype(BF16)
    c_lo = (ca - c_hi.astype(F32)).astype(BF16)
    w = w_ref[0]
    w_hi = w.astype(BF16)
    w_lo = (w - w_hi.astype(F32)).astype(BF16)
    o_ref[0] = _dot(c_hi, w_hi) + (_dot(c_lo, w_hi) + _dot(c_hi, w_lo)) + b_ref[0]


def _modulation(c_pad, w, b):
    nl, d, n = w.shape
    tn = 1024
    return pl.pallas_call(
        _mod_kernel,
        grid=(nl, n // tn),
        in_specs=[
            pl.BlockSpec((8, d), lambda l, j: (0, 0)),
            pl.BlockSpec((1, d, tn), lambda l, j: (l, 0, j)),
            pl.BlockSpec((1, 1, tn), lambda l, j: (l, 0, j)),
        ],
        out_specs=pl.BlockSpec((1, 8, tn), lambda l, j: (l, 0, j)),
        out_shape=jax.ShapeDtypeStruct((nl, 8, n), F32),
        compiler_params=_cparams("parallel", "parallel"),
        name="adaln_mod",
    )(c_pad, w, b.reshape(nl, 1, n))


def _rope_kernel(pos_ref, cos_ref, sin_ref):
    half = QK_ROPE // 2
    pos = pos_ref[0].astype(F32)
    lane = lax.broadcasted_iota(jnp.int32, (1, LANE), 1)
    idx = jnp.where(lane < half, lane, lane - half).astype(F32)
    inv_freq = jnp.exp(idx * (-jnp.log(ROPE_BASE) / half))
    ang = pos * inv_freq
    valid = lane < QK_ROPE
    cos_ref[0] = jnp.where(valid, jnp.cos(ang), 0.0)
    sin_ref[0] = jnp.where(valid, jnp.where(lane < half, -jnp.sin(ang), jnp.sin(ang)), 0.0)


def _rope_tables(positions):
    b, s = positions.shape
    ts = 512
    out = jax.ShapeDtypeStruct((b, s, LANE), F32)
    return pl.pallas_call(
        _rope_kernel,
        grid=(b, s // ts),
        in_specs=[pl.BlockSpec((1, ts, 1), lambda i, j: (i, j, 0))],
        out_specs=[pl.BlockSpec((1, ts, LANE), lambda i, j: (i, j, 0))] * 2,
        out_shape=[out, out],
        compiler_params=_cparams("parallel", "parallel"),
        name="rope_tables",
    )(positions.reshape(b, s, 1))


def _rotate(t, cos, sin):
    half = QK_ROPE // 2
    lane = lax.broadcasted_iota(jnp.int32, t.shape, 1)
    swapped = jnp.where(lane < half, pltpu.roll(t, LANE - half, 1), pltpu.roll(t, half, 1))
    return t * cos + swapped * sin


def _ffn_kernel(*refs, pre):
    if pre:
        (x_ref, y_ref, wo_ref, gmix_ref, g_ref, sh_ref, sc_ref, gate_ref,
         wg_ref, wu_ref, wd_ref, out_ref) = refs
    else:
        x_ref, g_ref, sh_ref, sc_ref, gate_ref, wg_ref, wu_ref, wd_ref, out_ref = refs
    x = x_ref[0]
    if pre:
        x = x + gmix_ref[0] * _dot(y_ref[0], wo_ref[...])
    hb = _modulated(x, g_ref[...], sh_ref[0], sc_ref[0]).astype(BF16)
    gate = _dot(hb, wg_ref[...])
    up = _dot(hb, wu_ref[...])
    act = (_silu(gate) * up).astype(BF16)
    out_ref[0] = x + (0.5 * gate_ref[0]) * _dot(act, wd_ref[...])


def _ffn(x, norm_g, shift, scale, gate, w_gate, w_up, w_down, pre=None, tm=512):
    b, s, d = x.shape
    f = w_gate.shape[1]
    tile = pl.BlockSpec((1, tm, d), lambda i, j: (i, j, 0))
    vec = pl.BlockSpec((1, 1, d), lambda i, j: (i, 0, 0))
    args, specs = [x], [tile]
    if pre is not None:
        y, w_o, g_mix = pre
        args += [y, w_o, g_mix]
        specs += [pl.BlockSpec((1, tm, y.shape[-1]), lambda i, j: (i, j, 0)), _const_spec(w_o.shape), vec]
    args += [norm_g.reshape(1, d), shift, scale, gate, w_gate, w_up, w_down]
    specs += [_const_spec((1, d)), vec, vec, vec,
              _const_spec((d, f)), _const_spec((d, f)), _const_spec((f, d))]
    return pl.pallas_call(
        functools.partial(_ffn_kernel, pre=pre is not None),
        grid=(b, s // tm),
        in_specs=specs,
        out_specs=tile,
        out_shape=jax.ShapeDtypeStruct((b, s, d), F32),
        compiler_params=_cparams("parallel", "parallel"),
        name="ffn_pre" if pre is not None else "ffn",
    )(*args)


def _gdn_in_kernel(x_ref, g_ref, sh_ref, sc_ref, w_ref, wba_ref, qkv_ref, z_ref, ba_ref):
    hb = _modulated(x_ref[0], g_ref[...], sh_ref[0], sc_ref[0]).astype(BF16)
    width = z_ref.shape[-1]
    p = _dot(hb, w_ref[...])
    qkv_ref[0] = p[:, :3 * width]
    z_ref[0] = p[:, 3 * width:]
    ba_ref[0] = _dot(hb, wba_ref[...])


def _gdn_in(x, norm_g, shift, scale, w_qkvz, w_ba, tm=512):
    b, s, d = x.shape
    width = w_qkvz.shape[1] // 4
    tile = lambda n: pl.BlockSpec((1, tm, n), lambda i, j: (i, j, 0))
    vec = pl.BlockSpec((1, 1, d), lambda i, j: (i, 0, 0))
    return pl.pallas_call(
        _gdn_in_kernel,
        grid=(b, s // tm),
        in_specs=[tile(d), _const_spec((1, d)), vec, vec, _const_spec(w_qkvz.shape), _const_spec(w_ba.shape)],
        out_specs=[tile(3 * width), tile(width), tile(2 * LANE)],
        out_shape=[jax.ShapeDtypeStruct((b, s, 3 * width), F32),
                   jax.ShapeDtypeStruct((b, s, width), F32),
                   jax.ShapeDtypeStruct((b, s, 2 * LANE), F32)],
        compiler_params=_cparams("parallel", "parallel"),
        name="gdn_in",
    )(x, norm_g.reshape(1, d), shift, scale, w_qkvz, w_ba)


def _split2(t):
    hi = t.astype(BF16)
    return hi, (t - hi.astype(F32)).astype(BF16)


def _dot2(pieces, m):
    return _dot(pieces[0], m) + _dot(pieces[1], m)


def _dot2_left(m, pieces):
    return _dot(m, pieces[0]) + _dot(m, pieces[1])


def _tile_rows(t, n):
    return jnp.concatenate([t] * n, axis=0)


def _iota2(shape):
    return lax.broadcasted_iota(jnp.int32, shape, 0), lax.broadcasted_iota(jnp.int32, shape, 1)


GDN_GROUP = 4


def _gdn_intra_kernel(qkv_ref, ba_ref, convw_ref, alog_ref, dtb_ref,
                      u_ref, w_ref, qe_ref, kd_ref, attn_ref, egl_ref, ext_ref, *, heads, nchunk):
    c = CHUNK
    dk = GDN_HEAD_DIM
    width = heads * dk
    ts = nchunk * c
    gw = GDN_GROUP * dk
    pw_ = GDN_GROUP * c
    ngroup = heads // GDN_GROUP

    @pl.when(pl.program_id(1) == 0)
    def _():
        ext_ref[0:8, :] = jnp.zeros((8, 3 * width), F32)

    cur = qkv_ref[0]
    ext_ref[8:8 + ts, :] = cur
    k_taps = convw_ref.shape[0]
    acc = ext_ref[8:8 + ts, :] * convw_ref[k_taps - 1:k_taps, :]
    for j in range(k_taps - 1):
        off = 8 - (k_taps - 1) + j
        acc = acc + ext_ref[off:off + ts, :] * convw_ref[j:j + 1, :]
    ext_ref[0:8, :] = cur[ts - 8:ts, :]
    act = _silu(acc)

    def l2n(t, scale):
        parts = []
        for h in range(heads):
            th = t[:, h * dk:(h + 1) * dk]
            parts.append(th * (lax.rsqrt(jnp.sum(th * th, axis=-1, keepdims=True) + EPS) * scale))
        return jnp.concatenate(parts, axis=1)

    qn = l2n(act[:, :width], dk ** -0.5)
    kn = l2n(act[:, width:2 * width], 1.0)
    v16 = act[:, 2 * width:].astype(BF16)
    k16 = kn.astype(BF16)
    q16 = qn.astype(BF16)

    ba = ba_ref[0]
    beta = 1.0 / (1.0 + jnp.exp(-ba[:, :LANE]))
    al = ba[:, LANE:] + dtb_ref[...]
    softplus = jnp.maximum(al, 0.0) + jnp.log1p(jnp.exp(-jnp.abs(al)))
    lane = lax.broadcasted_iota(jnp.int32, (ts, LANE), 1)
    gl = jnp.where(lane < heads, -jnp.exp(alog_ref[...]) * softplus, 0.0)

    r, cc = _iota2((ts, ts))
    tri = ((r // c == cc // c) & (r >= cc)).astype(BF16)
    gcum = _dot2_left(tri, _split2(gl))
    r, cc = _iota2((LANE, width))
    e_full = (r == cc // dk).astype(BF16)
    r, cc = _iota2((LANE, heads * c))
    e_pack = (r == cc // c).astype(BF16)
    gc_pieces = _split2(gcum)
    gc_full = _dot2(gc_pieces, e_full)
    gc_pack = _dot2(gc_pieces, e_pack)
    beta_pack = _dot(beta.astype(BF16), e_pack)

    qe_ref[0] = (qn * jnp.exp(gc_full)).astype(qe_ref.dtype)

    r, cc = _iota2((c, pw_))
    eye_p = r == cc % c
    incl_p = r >= cc % c
    strict_p = r > cc % c
    r, cc = _iota2((pw_, gw))
    bd_wide = r // c == cc // dk
    r, cc = _iota2((pw_, pw_))
    bd_sq = r // c == cc // c
    zero16 = jnp.zeros((), BF16)

    units = [(ci, g) for ci in range(nchunk) for g in range(ngroup)]
    rows = lambda ci: slice(ci * c, (ci + 1) * c)
    gsl = lambda g: slice(g * gw, (g + 1) * gw)
    psl = lambda g: slice(g * pw_, (g + 1) * pw_)

    for ci in range(nchunk):
        g_last = gc_full[ci * c + c - 1:ci * c + c, :]
        egl_ref[0, ci] = jnp.exp(g_last)
        kd_ref[0, rows(ci), :] = (kn[rows(ci)] * jnp.exp(g_last - gc_full[rows(ci)])).astype(kd_ref.dtype)

    bd_k = [jnp.where(bd_wide, _tile_rows(k16[rows(ci), gsl(g)], GDN_GROUP), zero16) for ci, g in units]
    sc = [_dot_nt(jnp.concatenate([k16[rows(ci), gsl(g)], q16[rows(ci), gsl(g)]], axis=0), bd_k[i])
          for i, (ci, g) in enumerate(units)]
    a_p, beta_r, eg_r = [], [], []
    for i, (ci, g) in enumerate(units):
        gcp = gc_pack[rows(ci), psl(g)]
        gr = jnp.sum(jnp.where(eye_p, gcp, 0.0), axis=0, keepdims=True)
        bp = beta_pack[rows(ci), psl(g)]
        beta_r.append(jnp.sum(jnp.where(eye_p, bp, 0.0), axis=0, keepdims=True))
        eg_r.append(jnp.exp(gr))
        decay = jnp.where(incl_p, jnp.exp(jnp.where(incl_p, gcp - gr, 0.0)), 0.0)
        a_p.append(jnp.where(strict_p, sc[i][:c] * bp * decay, 0.0))
        attn_ref[0, rows(ci), psl(g)] = jnp.where(incl_p, sc[i][c:] * decay, 0.0).astype(attn_ref.dtype)

    def bd(t16):
        return jnp.where(bd_sq, _tile_rows(t16, GDN_GROUP), zero16)

    t_inv = [jnp.where(eye_p, 1.0, 0.0) - a for a in a_p]
    a16 = [a.astype(BF16) for a in a_p]
    pw = [_dot(a16[i], bd(a16[i])) for i in range(len(units))]
    n_sq = c.bit_length() - 2
    for step in range(n_sq):
        pw16 = [p.astype(BF16) for p in pw]
        if step < n_sq - 1:
            prod = [_dot(jnp.concatenate([t_inv[i].astype(BF16), pw16[i]], axis=0), bd(pw16[i]))
                    for i in range(len(units))]
            t_inv = [t_inv[i] + prod[i][:c] for i in range(len(units))]
            pw = [prod[i][c:] for i in range(len(units))]
        else:
            t_inv = [t_inv[i] + _dot(t_inv[i].astype(BF16), bd(pw16[i])) for i in range(len(units))]

    for i, (ci, g) in enumerate(units):
        t_u = (t_inv[i] * beta_r[i]).astype(BF16)
        t_w = (t_inv[i] * (beta_r[i] * eg_r[i])).astype(BF16)
        bd_v = jnp.where(bd_wide, _tile_rows(v16[rows(ci), gsl(g)], GDN_GROUP), zero16)
        u_ref[0, rows(ci), gsl(g)] = _dot(t_u, bd_v).astype(u_ref.dtype)
        w_ref[0, rows(ci), gsl(g)] = _dot(t_w, bd_k[i]).astype(w_ref.dtype)


def _gdn_seq_kernel(u_ref, w_ref, qe_ref, kd_ref, attn_ref, egl_ref, z_ref, ng_ref, o_ref, state_ref,
                    *, heads, nchunk):
    c = CHUNK
    dk = GDN_HEAD_DIM
    gw = GDN_GROUP * dk
    pw_ = GDN_GROUP * c
    ngroup = heads // GDN_GROUP

    @pl.when(pl.program_id(1) == 0)
    def _():
        state_ref[...] = jnp.zeros_like(state_ref)

    r, cc = _iota2((pw_, gw))
    bd_wide = r // c == cc // dk
    zero16 = jnp.zeros((), BF16)
    hsl = lambda h: slice(h * dk, (h + 1) * dk)
    for ci in range(nchunk):
        rows = slice(ci * c, (ci + 1) * c)
        st16 = [state_ref[h].astype(BF16) for h in range(heads)]
        rd = [_dot(jnp.concatenate([w_ref[0, rows, hsl(h)], qe_ref[0, rows, hsl(h)]], axis=0), st16[h])
              for h in range(heads)]
        vn16 = [(u_ref[0, rows, hsl(h)] - rd[h][:c]).astype(BF16) for h in range(heads)]
        intra = []
        for g in range(ngroup):
            vg = jnp.concatenate(vn16[g * GDN_GROUP:(g + 1) * GDN_GROUP], axis=1)
            bd_vn = jnp.where(bd_wide, _tile_rows(vg, GDN_GROUP), zero16)
            intra.append(_dot(attn_ref[0, rows, g * pw_:(g + 1) * pw_], bd_vn))
        for h in range(heads):
            state_ref[h] = state_ref[h] * egl_ref[0, ci, :, hsl(h)] + _dot_tn(kd_ref[0, rows, hsl(h)], vn16[h])
        for h in range(heads):
            g, hh = divmod(h, GDN_GROUP)
            o = rd[h][c:] + intra[g][:, hh * dk:(hh + 1) * dk]
            o = o * lax.rsqrt(jnp.mean(o * o, axis=-1, keepdims=True) + EPS) * ng_ref[...]
            o_ref[0, rows, hsl(h)] = (o * _silu(z_ref[0, rows, hsl(h)])).astype(o_ref.dtype)


def _gdn_core(qkv, z, ba, conv_w, a_log, dt_bias, norm_g, nchunk=4):
    b, s, width = z.shape
    heads = width // GDN_HEAD_DIM
    ts = nchunk * CHUNK
    pad = lambda t: jnp.pad(t.reshape(1, heads), ((0, 0), (0, LANE - heads)))
    tile = lambda n: pl.BlockSpec((1, ts, n), lambda i, j: (i, j, 0))
    egl_spec = pl.BlockSpec((1, nchunk, 1, width), lambda i, j: (i, j, 0, 0))
    packed = heads * CHUNK
    u, w, qe, kd, attn, egl = pl.pallas_call(
        functools.partial(_gdn_intra_kernel, heads=heads, nchunk=nchunk),
        grid=(b, s // ts),
        in_specs=[tile(3 * width), tile(2 * LANE), _const_spec(conv_w.shape),
                  _const_spec((1, LANE)), _const_spec((1, LANE))],
        out_specs=[tile(width), tile(width), tile(width), tile(width), tile(packed), egl_spec],
        out_shape=[jax.ShapeDtypeStruct((b, s, width), F32),
                   jax.ShapeDtypeStruct((b, s, width), BF16),
                   jax.ShapeDtypeStruct((b, s, width), BF16),
                   jax.ShapeDtypeStruct((b, s, width), BF16),
                   jax.ShapeDtypeStruct((b, s, packed), BF16),
                   jax.ShapeDtypeStruct((b, s // CHUNK, 1, width), F32)],
        scratch_shapes=[pltpu.VMEM((8 + ts, 3 * width), F32)],
        compiler_params=_cparams("parallel", "arbitrary"),
        name="gdn_intra",
    )(qkv, ba, conv_w, pad(a_log), pad(dt_bias))
    return pl.pallas_call(
        functools.partial(_gdn_seq_kernel, heads=heads, nchunk=nchunk),
        grid=(b, s // ts),
        in_specs=[tile(width), tile(width), tile(width), tile(width), tile(packed), egl_spec,
                  tile(width), _const_spec((1, GDN_HEAD_DIM))],
        out_specs=tile(width),
        out_shape=jax.ShapeDtypeStruct((b, s, width), BF16),
        scratch_shapes=[pltpu.VMEM((heads, GDN_HEAD_DIM, GDN_HEAD_DIM), F32)],
        compiler_params=_cparams("parallel", "arbitrary"),
        name="gdn_seq",
    )(u, w, qe, kd, attn, egl, z, norm_g.reshape(1, GDN_HEAD_DIM))


def _kv_kernel(x_ref, g_ref, sh_ref, sc_ref, wdl_ref, wdr_ref, lg_ref, wk_ref, wv_ref,
               kgn_ref, kgr_ref, cos_ref, sin_ref, k_ref, vt_ref, *, heads):
    hb = _modulated(x_ref[0], g_ref[...], sh_ref[0], sc_ref[0]).astype(BF16)
    lat = _dot(hb, wdl_ref[...])
    rope = _dot(hb, wdr_ref[...])
    lat = lat * lax.rsqrt(jnp.mean(lat * lat, axis=-1, keepdims=True) + EPS) * lg_ref[...]
    lat16 = lat.astype(BF16)
    k_nope = _dot(lat16, wk_ref[...])
    v = _dot(lat16, wv_ref[...])
    rope_sq = jnp.sum(rope * rope, axis=-1, keepdims=True)
    for h in range(heads):
        sl = slice(h * QK_NOPE, (h + 1) * QK_NOPE)
        kn = k_nope[:, sl]
        inv = lax.rsqrt((jnp.sum(kn * kn, axis=-1, keepdims=True) + rope_sq) / (QK_NOPE + QK_ROPE) + EPS)
        k_ref[0, h, :, :QK_NOPE] = (kn * inv * kgn_ref[...]).astype(k_ref.dtype)
        kr = _rotate(rope * inv * kgr_ref[...], cos_ref[0], sin_ref[0])
        k_ref[0, h, :, QK_NOPE:] = kr.astype(k_ref.dtype)
        vt_ref[0, h, 0] = jnp.transpose(v[:, h * V_HEAD:(h + 1) * V_HEAD]).astype(vt_ref.dtype)


def _shared_kv(x, norm_g, shift, scale, w_dl, w_dr, lat_g, w_k, w_v, kg_nope, kg_rope, cos, sin, tm=512):
    b, s, d = x.shape
    heads = w_k.shape[1] // QK_NOPE
    tile = lambda n: pl.BlockSpec((1, tm, n), lambda i, j: (i, j, 0))
    vec = pl.BlockSpec((1, 1, d), lambda i, j: (i, 0, 0))
    return pl.pallas_call(
        functools.partial(_kv_kernel, heads=heads),
        grid=(b, s // tm),
        in_specs=[tile(d), _const_spec((1, d)), vec, vec, _const_spec(w_dl.shape), _const_spec(w_dr.shape),
                  _const_spec(lat_g.shape), _const_spec(w_k.shape), _const_spec(w_v.shape),
                  _const_spec(kg_nope.shape), _const_spec(kg_rope.shape), tile(LANE), tile(LANE)],
        out_specs=[pl.BlockSpec((1, heads, tm, QK_PAD), lambda i, j: (i, 0, j, 0)),
                   pl.BlockSpec((1, heads, 1, V_HEAD, tm), lambda i, j: (i, 0, j, 0, 0))],
        out_shape=[jax.ShapeDtypeStruct((b, heads, s, QK_PAD), BF16),
                   jax.ShapeDtypeStruct((b, heads, s // tm, V_HEAD, tm), BF16)],
        compiler_params=_cparams("parallel", "parallel"),
        name="mla_kv",
    )(x, norm_g.reshape(1, d), shift, scale, w_dl, w_dr, lat_g, w_k, w_v, kg_nope, kg_rope, cos, sin)


def _q_kernel(x_ref, g_ref, sh_ref, sc_ref, wdq_ref, qlg_ref, wqn_ref, wqr_ref,
              qgn_ref, qgr_ref, cos_ref, sin_ref, q_ref, *, heads):
    hb = _modulated(x_ref[0], g_ref[...], sh_ref[0], sc_ref[0]).astype(BF16)
    ql = _dot(hb, wdq_ref[...])
    ql = ql * lax.rsqrt(jnp.mean(ql * ql, axis=-1, keepdims=True) + EPS) * qlg_ref[...]
    ql16 = ql.astype(BF16)
    q_nope = _dot(ql16, wqn_ref[...])
    q_rope = _dot(ql16, wqr_ref[...])
    sm_scale = (QK_NOPE + QK_ROPE) ** -0.5
    for h in range(heads):
        sl = slice(h * LANE, (h + 1) * LANE)
        qn, qr = q_nope[:, sl], q_rope[:, sl]
        ssq = jnp.sum(qn * qn, axis=-1, keepdims=True) + jnp.sum(qr * qr, axis=-1, keepdims=True)
        inv = lax.rsqrt(ssq / (QK_NOPE + QK_ROPE) + EPS) * sm_scale
        q_ref[0, h, :, :QK_NOPE] = (qn * inv * qgn_ref[...]).astype(q_ref.dtype)
        q_ref[0, h, :, QK_NOPE:] = _rotate(qr * inv * qgr_ref[...], cos_ref[0], sin_ref[0]).astype(q_ref.dtype)


def _mla_q(x, norm_g, shift, scale, w_dq, ql_g, w_qn, w_qr, qg_nope, qg_rope, cos, sin, tm=512):
    b, s, d = x.shape
    heads = w_qn.shape[1] // QK_NOPE
    tile = lambda n: pl.BlockSpec((1, tm, n), lambda i, j: (i, j, 0))
    vec = pl.BlockSpec((1, 1, d), lambda i, j: (i, 0, 0))
    return pl.pallas_call(
        functools.partial(_q_kernel, heads=heads),
        grid=(b, s // tm),
        in_specs=[tile(d), _const_spec((1, d)), vec, vec, _const_spec(w_dq.shape), _const_spec(ql_g.shape),
                  _const_spec(w_qn.shape), _const_spec(w_qr.shape), _const_spec(qg_nope.shape),
                  _const_spec(qg_rope.shape), tile(LANE), tile(LANE)],
        out_specs=pl.BlockSpec((1, heads, tm, QK_PAD), lambda i, j: (i, 0, j, 0)),
        out_shape=jax.ShapeDtypeStruct((b, heads, s, QK_PAD), BF16),
        compiler_params=_cparams("parallel", "parallel"),
        name="mla_q",
    )(x, norm_g.reshape(1, d), shift, scale, w_dq, ql_g, w_qn, w_qr, qg_nope, qg_rope, cos, sin)


def _attn_kernel(q_ref, k_ref, vt_ref, o_ref, *, tq):
    qi = pl.program_id(2)
    q = q_ref[0, 0]

    def block(j, carry, masked):
        m, l, acc = carry
        kj = k_ref[0, 0, pl.ds(pl.multiple_of(j * tq, tq), tq), :]
        s = _dot_nt(kj, q)
        if masked:
            key_c = lax.broadcasted_iota(jnp.int32, (tq, tq), 0) // CHUNK
            qry_c = lax.broadcasted_iota(jnp.int32, (tq, tq), 1) // CHUNK
            s = jnp.where(key_c <= qry_c, s, -jnp.inf)
        m_new = jnp.maximum(m, jnp.max(s, axis=0, keepdims=True))
        alpha = jnp.exp(m - m_new)
        p = jnp.exp(s - m_new)
        l = alpha * l + jnp.sum(p, axis=0, keepdims=True)
        acc = alpha * acc + _dot(vt_ref[0, 0, j], p.astype(BF16))
        return m_new, l, acc

    init = (jnp.full((1, tq), -jnp.inf, F32), jnp.zeros((1, tq), F32), jnp.zeros((V_HEAD, tq), F32))
    carry = block(qi, init, True)
    m, l, acc = lax.fori_loop(0, qi, lambda j, cr: block(j, cr, False), carry)
    o_ref[0] = jnp.transpose(acc / l).astype(o_ref.dtype)


def _attention(q, k, vt):
    b, heads, s, dq = q.shape
    tq = vt.shape[-1]
    nkb = s // tq
    return pl.pallas_call(
        functools.partial(_attn_kernel, tq=tq),
        grid=(b, heads, s // tq),
        in_specs=[pl.BlockSpec((1, 1, tq, dq), lambda i, h, j: (i, h, j, 0)),
                  pl.BlockSpec((1, 1, s, dq), lambda i, h, j: (i, h, 0, 0)),
                  pl.BlockSpec((1, 1, nkb, V_HEAD, tq), lambda i, h, j: (i, h, 0, 0, 0))],
        out_specs=pl.BlockSpec((1, tq, V_HEAD), lambda i, h, j: (i, j, h)),
        out_shape=jax.ShapeDtypeStruct((b, s, heads * V_HEAD), BF16),
        compiler_params=_cparams("parallel", "parallel", "parallel"),
        name="mla_attn",
    )(q, k, vt)


def _pad_cols(w, n):
    return jnp.pad(w, ((0, 0), (0, n - w.shape[1])))


def kernel(x, c, positions, ada_w, ada_b, norm_g, ffn_w_in, ffn_w_out, gdn_w_in, gdn_conv_w, gdn_a_log,
           gdn_dt_bias, gdn_norm_g, gdn_w_out, kv_ada_w, kv_ada_b, kv_norm_g, mla_w_dkv, mla_kv_norm_g,
           mla_w_ukv, mla_k_norm_g, mla_w_dq, mla_q_lora_norm_g, mla_w_uq, mla_q_norm_g, mla_w_out):
    b, s, d = x.shape
    depth = ada_w.shape[0]
    n_a = gdn_w_in.shape[0]
    n_mod = ada_w.shape[2] // d
    d_ff = ffn_w_out.shape[2]
    width = gdn_w_out.shape[1]
    g_heads = width // GDN_HEAD_DIM
    kv_lora = mla_kv_norm_g.shape[0]
    m_heads = mla_w_ukv.shape[1] // (QK_NOPE + V_HEAD)

    c_pad = jnp.pad(c, ((0, 8 - b), (0, 0)))
    mod = _modulation(c_pad, ada_w, ada_b)[:, :b].reshape(depth, b, n_mod, 1, d)
    kv_mod = _modulation(c_pad, kv_ada_w[None], kv_ada_b[None])[0, :b].reshape(b, 2, 1, d)
    cos, sin = _rope_tables(positions)

    k_sh = vt_sh = None
    for l in range(depth):
        m = lambda i: mod[l, :, i]
        w_in = ffn_w_in[l].astype(BF16)
        w_out = ffn_w_out[l].astype(BF16)
        x = _ffn(x, norm_g[l, 0], m(0), m(1), m(2), w_in[0, :, :d_ff], w_in[0, :, d_ff:], w_out[0])
        if l < n_a:
            w = gdn_w_in[l]
            w_ba = jnp.concatenate([_pad_cols(w[:, 4 * width:4 * width + g_heads], LANE),
                                    _pad_cols(w[:, 4 * width + g_heads:], LANE)], axis=1).astype(BF16)
            qkv, z, ba = _gdn_in(x, norm_g[l, 1], m(3), m(4), w[:, :4 * width].astype(BF16), w_ba)
            y = _gdn_core(qkv, z, ba, gdn_conv_w[l], gdn_a_log[l], gdn_dt_bias[l], gdn_norm_g[l])
            w_o = gdn_w_out[l].astype(BF16)
        else:
            j = l - n_a
            w_uq = mla_w_uq[j].reshape(-1, m_heads, QK_NOPE + QK_ROPE)
            w_qn = w_uq[:, :, :QK_NOPE].reshape(-1, m_heads * QK_NOPE).astype(BF16)
            w_qr = jnp.pad(w_uq[:, :, QK_NOPE:], ((0, 0), (0, 0), (0, LANE - QK_ROPE)))
            w_qr = w_qr.reshape(-1, m_heads * LANE).astype(BF16)
            qg = mla_q_norm_g[j]
            q = _mla_q(x, norm_g[l, 1], m(3), m(4), mla_w_dq[j].astype(BF16),
                       mla_q_lora_norm_g[j].reshape(1, -1), w_qn, w_qr,
                       qg[:QK_NOPE].reshape(1, -1), _pad_cols(qg[QK_NOPE:].reshape(1, -1), LANE), cos, sin)
            y = _attention(q, k_sh, vt_sh)
            w_o = mla_w_out[j].astype(BF16)
        x = _ffn(x, norm_g[l, 2], m(6), m(7), m(8), w_in[1, :, :d_ff], w_in[1, :, d_ff:], w_out[1],
                 pre=(y, w_o, m(5)))
        if l == n_a - 1:
            w_ukv = mla_w_ukv.reshape(kv_lora, m_heads, QK_NOPE + V_HEAD)
            w_k = w_ukv[:, :, :QK_NOPE].reshape(kv_lora, m_heads * QK_NOPE).astype(BF16)
            w_v = w_ukv[:, :, QK_NOPE:].reshape(kv_lora, m_heads * V_HEAD).astype(BF16)
            kg = mla_k_norm_g
            k_sh, vt_sh = _shared_kv(
                x, kv_norm_g, kv_mod[:, 0], kv_mod[:, 1], mla_w_dkv[:, :kv_lora].astype(BF16),
                _pad_cols(mla_w_dkv[:, kv_lora:], LANE).astype(BF16), mla_kv_norm_g.reshape(1, -1), w_k, w_v,
                kg[:QK_NOPE].reshape(1, -1), _pad_cols(kg[QK_NOPE:].reshape(1, -1), LANE), cos, sin)
    return x
```

```python
import functools

import jax
import jax.numpy as jnp
from jax import lax
from jax.experimental import pallas as pl
from jax.experimental.pallas import tpu as pltpu

F32 = jnp.float32
BF16 = jnp.bfloat16

EPS = 1e-6
CHUNK = 64
ROPE_BASE = 10000.0
LANE = 128
VMEM_LIMIT = 56 * 1024 * 1024

GDN_HEAD_DIM = 128
QK_NOPE = 128
QK_ROPE = 64
V_HEAD = 128
QK_PAD = 256


def _cparams(*sem):
    return pltpu.CompilerParams(dimension_semantics=sem, vmem_limit_bytes=VMEM_LIMIT)


def _silu(t):
    return t / (1.0 + jnp.exp(-t))


def _dot(a, b):
    return jnp.dot(a, b, preferred_element_type=F32)


def _dot_nt(a, b):
    return lax.dot_general(a, b, (((1,), (1,)), ((), ())), preferred_element_type=F32)


def _dot_tn(a, b):
    return lax.dot_general(a, b, (((0,), (0,)), ((), ())), preferred_element_type=F32)


def _split3(t):
    hi = t.astype(BF16)
    r1 = t - hi.astype(F32)
    mid = r1.astype(BF16)
    lo = (r1 - mid.astype(F32)).astype(BF16)
    return hi, mid, lo


def _const_spec(shape):
    nd = len(shape)
    return pl.BlockSpec(shape, lambda *_: (0,) * nd, pipeline_mode=pl.Buffered(1))


def _pick_spec(lead, block, tail=None):
    index = tuple(lead) + tuple(tail or (0,) * len(block))
    return pl.BlockSpec((None,) * len(lead) + tuple(block), lambda *_: index, pipeline_mode=pl.Buffered(1))


def _modulated(x, g, shift, scale):
    ms = jnp.mean(x * x, axis=-1, keepdims=True)
    return (x * lax.rsqrt(ms + EPS) * g) * (1.0 + scale) + shift


def _mod_kernel(c_ref, w_ref, b_ref, o_ref):
    ca = _silu(c_ref[...])
    c_hi = ca.astype(BF16)
    c_lo = (ca - c_hi.astype(F32)).astype(BF16)
    w = w_ref[0]
    w_hi = w.astype(BF16)
    w_lo = (w - w_hi.astype(F32)).astype(BF16)
    o_ref[0] = _dot(c_hi, w_hi) + (_dot(c_lo, w_hi) + _dot(c_hi, w_lo)) + b_ref[0]


def _modulation(c_pad, w, b):
    nl, d, n = w.shape
    tn = 1024
    return pl.pallas_call(
        _mod_kernel,
        grid=(nl, n // tn),
        in_specs=[
            pl.BlockSpec((8, d), lambda l, j: (0, 0)),
            pl.BlockSpec((1, d, tn), lambda l, j: (l, 0, j)),
            pl.BlockSpec((1, 1, tn), lambda l, j: (l, 0, j)),
        ],
        out_specs=pl.BlockSpec((1, 8, tn), lambda l, j: (l, 0, j)),
        out_shape=jax.ShapeDtypeStruct((nl, 8, n), F32),
        compiler_params=_cparams("parallel", "parallel"),
        name="adaln_mod",
    )(c_pad, w, b.reshape(nl, 1, n))


def _rope_kernel(pos_ref, cos_ref, sin_ref):
    half = QK_ROPE // 2
    pos = pos_ref[0].astype(F32)
    lane = lax.broadcasted_iota(jnp.int32, (1, LANE), 1)
    idx = jnp.where(lane < half, lane, lane - half).astype(F32)
    inv_freq = jnp.exp(idx * (-jnp.log(ROPE_BASE) / half))
    ang = pos * inv_freq
    valid = lane < QK_ROPE
    cos_ref[0] = jnp.where(valid, jnp.cos(ang), 0.0)
    sin_ref[0] = jnp.where(valid, jnp.where(lane < half, -jnp.sin(ang), jnp.sin(ang)), 0.0)


def _rope_tables(positions):
    b, s = positions.shape
    ts = 512
    out = jax.ShapeDtypeStruct((b, s, LANE), F32)
    return pl.pallas_call(
        _rope_kernel,
        grid=(b, s // ts),
        in_specs=[pl.BlockSpec((1, ts, 1), lambda i, j: (i, j, 0))],
        out_specs=[pl.BlockSpec((1, ts, LANE), lambda i, j: (i, j, 0))] * 2,
        out_shape=[out, out],
        compiler_params=_cparams("parallel", "parallel"),
        name="rope_tables",
    )(positions.reshape(b, s, 1))


def _rotate(t, cos, sin):
    half = QK_ROPE // 2
    lane = lax.broadcasted_iota(jnp.int32, t.shape, 1)
    swapped = jnp.where(lane < half, pltpu.roll(t, LANE - half, 1), pltpu.roll(t, half, 1))
    return t * cos + swapped * sin


def _ffn_kernel(*refs, pre):
    if pre:
        (x_ref, y_ref, wo_ref, gmix_ref, g_ref, sh_ref, sc_ref, gate_ref,
         wg_ref, wu_ref, wd_ref, out_ref) = refs
    else:
        x_ref, g_ref, sh_ref, sc_ref, gate_ref, wg_ref, wu_ref, wd_ref, out_ref = refs
    x = x_ref[0]
    if pre:
        x = x + gmix_ref[0] * _dot(y_ref[0], wo_ref[...])
    hb = _modulated(x, g_ref[...], sh_ref[0], sc_ref[0]).astype(BF16)
    gate = _dot(hb, wg_ref[...])
    up = _dot(hb, wu_ref[...])
    act = (_silu(gate) * up).astype(BF16)
    out_ref[0] = x + (0.5 * gate_ref[0]) * _dot(act, wd_ref[...])


def _ffn(x, norm_g, shift, scale, gate, w_in, w_out, idx, pre=None, tm=512):
    b, s, d = x.shape
    f = w_out.shape[-2]
    tile = pl.BlockSpec((1, tm, d), lambda i, j: (i, j, 0))
    vec = pl.BlockSpec((1, 1, d), lambda i, j: (i, 0, 0))
    args, specs = [x], [tile]
    if pre is not None:
        y, w_o, jo, g_mix = pre
        args += [y, w_o, g_mix]
        specs += [pl.BlockSpec((1, tm, y.shape[-1]), lambda i, j: (i, j, 0)), _pick_spec((jo,), w_o.shape[1:]), vec]
    args += [norm_g.reshape(1, d), shift, scale, gate, w_in, w_in, w_out]
    specs += [_const_spec((1, d)), vec, vec, vec,
              _pick_spec(idx, (d, f), (0, 0)), _pick_spec(idx, (d, f), (0, 1)), _pick_spec(idx, (f, d))]
    return pl.pallas_call(
        functools.partial(_ffn_kernel, pre=pre is not None),
        grid=(b, s // tm),
        in_specs=specs,
        out_specs=tile,
        out_shape=jax.ShapeDtypeStruct((b, s, d), F32),
        compiler_params=_cparams("parallel", "parallel"),
        name="ffn_pre" if pre is not None else "ffn",
    )(*args)


def _gdn_in_kernel(x_ref, g_ref, sh_ref, sc_ref, w_ref, wba_ref, qkv_ref, z_ref, ba_ref):
    hb = _modulated(x_ref[0], g_ref[...], sh_ref[0], sc_ref[0]).astype(BF16)
    width = z_ref.shape[-1]
    p = _dot(hb, w_ref[...])
    qkv_ref[0] = p[:, :3 * width]
    z_ref[0] = p[:, 3 * width:]
    ba_ref[0] = _dot(hb, wba_ref[...])


def _gdn_in(x, norm_g, shift, scale, w_all, l, width, w_ba, tm=512):
    b, s, d = x.shape
    tile = lambda n: pl.BlockSpec((1, tm, n), lambda i, j: (i, j, 0))
    vec = pl.BlockSpec((1, 1, d), lambda i, j: (i, 0, 0))
    return pl.pallas_call(
        _gdn_in_kernel,
        grid=(b, s // tm),
        in_specs=[tile(d), _const_spec((1, d)), vec, vec, _pick_spec((l,), (d, 4 * width)), _const_spec(w_ba.shape)],
        out_specs=[tile(3 * width), tile(width), tile(2 * LANE)],
        out_shape=[jax.ShapeDtypeStruct((b, s, 3 * width), F32),
                   jax.ShapeDtypeStruct((b, s, width), F32),
                   jax.ShapeDtypeStruct((b, s, 2 * LANE), F32)],
        compiler_params=_cparams("parallel", "parallel"),
        name="gdn_in",
    )(x, norm_g.reshape(1, d), shift, scale, w_all, w_ba)


def _split2(t):
    hi = t.astype(BF16)
    return hi, (t - hi.astype(F32)).astype(BF16)


def _dot2(pieces, m):
    return _dot(pieces[0], m) + _dot(pieces[1], m)


def _dot2_left(m, pieces):
    return _dot(m, pieces[0]) + _dot(m, pieces[1])


def _tile_rows(t, n):
    return jnp.concatenate([t] * n, axis=0)


def _iota2(shape):
    return lax.broadcasted_iota(jnp.int32, shape, 0), lax.broadcasted_iota(jnp.int32, shape, 1)


GDN_GROUP = 4


def _gdn_intra_kernel(qkv_ref, ba_ref, convw_ref, alog_ref, dtb_ref,
                      u_ref, w_ref, qe_ref, kd_ref, attn_ref, egl_ref, ext_ref, *, heads, nchunk):
    c = CHUNK
    dk = GDN_HEAD_DIM
    width = heads * dk
    ts = nchunk * c
    gw = GDN_GROUP * dk
    pw_ = GDN_GROUP * c
    ngroup = heads // GDN_GROUP

    @pl.when(pl.program_id(1) == 0)
    def _():
        ext_ref[0:8, :] = jnp.zeros((8, 3 * width), F32)

    cur = qkv_ref[0]
    ext_ref[8:8 + ts, :] = cur
    k_taps = convw_ref.shape[0]
    acc = ext_ref[8:8 + ts, :] * convw_ref[k_taps - 1:k_taps, :]
    for j in range(k_taps - 1):
        off = 8 - (k_taps - 1) + j
        acc = acc + ext_ref[off:off + ts, :] * convw_ref[j:j + 1, :]
    ext_ref[0:8, :] = cur[ts - 8:ts, :]
    act = _silu(acc)

    def l2n(t, scale):
        parts = []
        for h in range(heads):
            th = t[:, h * dk:(h + 1) * dk]
            parts.append(th * (lax.rsqrt(jnp.sum(th * th, axis=-1, keepdims=True) + EPS) * scale))
        return jnp.concatenate(parts, axis=1)

    qn = l2n(act[:, :width], dk ** -0.5)
    kn = l2n(act[:, width:2 * width], 1.0)
    v16 = act[:, 2 * width:].astype(BF16)
    k16 = kn.astype(BF16)
    q16 = qn.astype(BF16)

    ba = ba_ref[0]
    beta = 1.0 / (1.0 + jnp.exp(-ba[:, :LANE]))
    al = ba[:, LANE:] + dtb_ref[...]
    softplus = jnp.maximum(al, 0.0) + jnp.log1p(jnp.exp(-jnp.abs(al)))
    lane = lax.broadcasted_iota(jnp.int32, (ts, LANE), 1)
    gl = jnp.where(lane < heads, -jnp.exp(alog_ref[...]) * softplus, 0.0)

    r, cc = _iota2((ts, ts))
    tri = ((r // c == cc // c) & (r >= cc)).astype(BF16)
    gcum = _dot2_left(tri, _split2(gl))
    r, cc = _iota2((LANE, width))
    e_full = (r == cc // dk).astype(BF16)
    r, cc = _iota2((LANE, heads * c))
    e_pack = (r == cc // c).astype(BF16)
    gc_pieces = _split2(gcum)
    gc_full = _dot2(gc_pieces, e_full)
    gc_pack = _dot2(gc_pieces, e_pack)
    beta_pack = _dot(beta.astype(BF16), e_pack)

    qe_ref[0] = (qn * jnp.exp(gc_full)).astype(qe_ref.dtype)

    r, cc = _iota2((c, pw_))
    eye_p = r == cc % c
    incl_p = r >= cc % c
    strict_p = r > cc % c
    r, cc = _iota2((pw_, gw))
    bd_wide = r // c == cc // dk
    r, cc = _iota2((pw_, pw_))
    bd_sq = r // c == cc // c
    zero16 = jnp.zeros((), BF16)

    units = [(ci, g) for ci in range(nchunk) for g in range(ngroup)]
    rows = lambda ci: slice(ci * c, (ci + 1) * c)
    gsl = lambda g: slice(g * gw, (g + 1) * gw)
    psl = lambda g: slice(g * pw_, (g + 1) * pw_)

    for ci in range(nchunk):
        g_last = gc_full[ci * c + c - 1:ci * c + c, :]
        egl_ref[0, ci] = jnp.exp(g_last)
        kd_ref[0, rows(ci), :] = (kn[rows(ci)] * jnp.exp(g_last - gc_full[rows(ci)])).astype(kd_ref.dtype)

    bd_k = [jnp.where(bd_wide, _tile_rows(k16[rows(ci), gsl(g)], GDN_GROUP), zero16) for ci, g in units]
    sc = [_dot_nt(jnp.concatenate([k16[rows(ci), gsl(g)], q16[rows(ci), gsl(g)]], axis=0), bd_k[i])
          for i, (ci, g) in enumerate(units)]
    a_p, beta_r, eg_r = [], [], []
    for i, (ci, g) in enumerate(units):
        gcp = gc_pack[rows(ci), psl(g)]
        gr = jnp.sum(jnp.where(eye_p, gcp, 0.0), axis=0, keepdims=True)
        bp = beta_pack[rows(ci), psl(g)]
        beta_r.append(jnp.sum(jnp.where(eye_p, bp, 0.0), axis=0, keepdims=True))
        eg_r.append(jnp.exp(gr))
        decay = jnp.where(incl_p, jnp.exp(jnp.where(incl_p, gcp - gr, 0.0)), 0.0)
        a_p.append(jnp.where(strict_p, sc[i][:c] * bp * decay, 0.0))
        attn_ref[0, rows(ci), psl(g)] = jnp.where(incl_p, sc[i][c:] * decay, 0.0).astype(attn_ref.dtype)

    def bd(t16):
        return jnp.where(bd_sq, _tile_rows(t16, GDN_GROUP), zero16)

    t_inv = [jnp.where(eye_p, 1.0, 0.0) - a for a in a_p]
    a16 = [a.astype(BF16) for a in a_p]
    pw = [_dot(a16[i], bd(a16[i])) for i in range(len(units))]
    n_sq = c.bit_length() - 2
    for step in range(n_sq):
        pw16 = [p.astype(BF16) for p in pw]
        if step < n_sq - 1:
            prod = [_dot(jnp.concatenate([t_inv[i].astype(BF16), pw16[i]], axis=0), bd(pw16[i]))
                    for i in range(len(units))]
            t_inv = [t_inv[i] + prod[i][:c] for i in range(len(units))]
            pw = [prod[i][c:] for i in range(len(units))]
        else:
            t_inv = [t_inv[i] + _dot(t_inv[i].astype(BF16), bd(pw16[i])) for i in range(len(units))]

    for i, (ci, g) in enumerate(units):
        t_u = (t_inv[i] * beta_r[i]).astype(BF16)
        t_w = (t_inv[i] * (beta_r[i] * eg_r[i])).astype(BF16)
        bd_v = jnp.where(bd_wide, _tile_rows(v16[rows(ci), gsl(g)], GDN_GROUP), zero16)
        u_ref[0, rows(ci), gsl(g)] = _dot(t_u, bd_v).astype(u_ref.dtype)
        w_ref[0, rows(ci), gsl(g)] = _dot(t_w, bd_k[i]).astype(w_ref.dtype)


def _gdn_seq_kernel(u_ref, w_ref, qe_ref, kd_ref, attn_ref, egl_ref, z_ref, ng_ref, o_ref, state_ref,
                    *, heads, nchunk):
    c = CHUNK
    dk = GDN_HEAD_DIM
    gw = GDN_GROUP * dk
    pw_ = GDN_GROUP * c
    ngroup = heads // GDN_GROUP

    @pl.when(pl.program_id(1) == 0)
    def _():
        state_ref[...] = jnp.zeros_like(state_ref)

    r, cc = _iota2((pw_, gw))
    bd_wide = r // c == cc // dk
    zero16 = jnp.zeros((), BF16)
    hsl = lambda h: slice(h * dk, (h + 1) * dk)
    for ci in range(nchunk):
        rows = slice(ci * c, (ci + 1) * c)
        st16 = [state_ref[h].astype(BF16) for h in range(heads)]
        rd = [_dot(jnp.concatenate([w_ref[0, rows, hsl(h)], qe_ref[0, rows, hsl(h)]], axis=0), st16[h])
              for h in range(heads)]
        vn16 = [(u_ref[0, rows, hsl(h)] - rd[h][:c]).astype(BF16) for h in range(heads)]
        intra = []
        for g in range(ngroup):
            vg = jnp.concatenate(vn16[g * GDN_GROUP:(g + 1) * GDN_GROUP], axis=1)
            bd_vn = jnp.where(bd_wide, _tile_rows(vg, GDN_GROUP), zero16)
            intra.append(_dot(attn_ref[0, rows, g * pw_:(g + 1) * pw_], bd_vn))
        for h in range(heads):
            state_ref[h] = state_ref[h] * egl_ref[0, ci, :, hsl(h)] + _dot_tn(kd_ref[0, rows, hsl(h)], vn16[h])
        for h in range(heads):
            g, hh = divmod(h, GDN_GROUP)
            o = rd[h][c:] + intra[g][:, hh * dk:(hh + 1) * dk]
            o = o * lax.rsqrt(jnp.mean(o * o, axis=-1, keepdims=True) + EPS) * ng_ref[...]
            o_ref[0, rows, hsl(h)] = (o * _silu(z_ref[0, rows, hsl(h)])).astype(o_ref.dtype)


def _gdn_core(qkv, z, ba, conv_w, a_log, dt_bias, norm_g, nchunk=4):
    b, s, width = z.shape
    heads = width // GDN_HEAD_DIM
    ts = nchunk * CHUNK
    pad = lambda t: jnp.pad(t.reshape(1, heads), ((0, 0), (0, LANE - heads)))
    tile = lambda n: pl.BlockSpec((1, ts, n), lambda i, j: (i, j, 0))
    egl_spec = pl.BlockSpec((1, nchunk, 1, width), lambda i, j: (i, j, 0, 0))
    packed = heads * CHUNK
    u, w, qe, kd, attn, egl = pl.pallas_call(
        functools.partial(_gdn_intra_kernel, heads=heads, nchunk=nchunk),
        grid=(b, s // ts),
        in_specs=[tile(3 * width), tile(2 * LANE), _const_spec(conv_w.shape),
                  _const_spec((1, LANE)), _const_spec((1, LANE))],
        out_specs=[tile(width), tile(width), tile(width), tile(width), tile(packed), egl_spec],
        out_shape=[jax.ShapeDtypeStruct((b, s, width), F32),
                   jax.ShapeDtypeStruct((b, s, width), BF16),
                   jax.ShapeDtypeStruct((b, s, width), BF16),
                   jax.ShapeDtypeStruct((b, s, width), BF16),
                   jax.ShapeDtypeStruct((b, s, packed), BF16),
                   jax.ShapeDtypeStruct((b, s // CHUNK, 1, width), F32)],
        scratch_shapes=[pltpu.VMEM((8 + ts, 3 * width), F32)],
        compiler_params=_cparams("parallel", "arbitrary"),
        name="gdn_intra",
    )(qkv, ba, conv_w, pad(a_log), pad(dt_bias))
    return pl.pallas_call(
        functools.partial(_gdn_seq_kernel, heads=heads, nchunk=nchunk),
        grid=(b, s // ts),
        in_specs=[tile(width), tile(width), tile(width), tile(width), tile(packed), egl_spec,
                  tile(width), _const_spec((1, GDN_HEAD_DIM))],
        out_specs=tile(width),
        out_shape=jax.ShapeDtypeStruct((b, s, width), BF16),
        scratch_shapes=[pltpu.VMEM((heads, GDN_HEAD_DIM, GDN_HEAD_DIM), F32)],
        compiler_params=_cparams("parallel", "arbitrary"),
        name="gdn_seq",
    )(u, w, qe, kd, attn, egl, z, norm_g.reshape(1, GDN_HEAD_DIM))


def _kv_kernel(x_ref, g_ref, sh_ref, sc_ref, wdl_ref, wdr_ref, lg_ref, wk_ref, wv_ref,
               kgn_ref, kgr_ref, cos_ref, sin_ref, k_ref, vt_ref, *, heads):
    hb = _modulated(x_ref[0], g_ref[...], sh_ref[0], sc_ref[0]).astype(BF16)
    lat = _dot(hb, wdl_ref[...])
    rope = _dot(hb, wdr_ref[...])
    lat = lat * lax.rsqrt(jnp.mean(lat * lat, axis=-1, keepdims=True) + EPS) * lg_ref[...]
    lat16 = lat.astype(BF16)
    k_nope = _dot(lat16, wk_ref[...])
    v = _dot(lat16, wv_ref[...])
    rope_sq = jnp.sum(rope * rope, axis=-1, keepdims=True)
    for h in range(heads):
        sl = slice(h * QK_NOPE, (h + 1) * QK_NOPE)
        kn = k_nope[:, sl]
        inv = lax.rsqrt((jnp.sum(kn * kn, axis=-1, keepdims=True) + rope_sq) / (QK_NOPE + QK_ROPE) + EPS)
        k_ref[0, h, :, :QK_NOPE] = (kn * inv * kgn_ref[...]).astype(k_ref.dtype)
        kr = _rotate(rope * inv * kgr_ref[...], cos_ref[0], sin_ref[0])
        k_ref[0, h, :, QK_NOPE:] = kr.astype(k_ref.dtype)
        vt_ref[0, h, 0] = jnp.transpose(v[:, h * V_HEAD:(h + 1) * V_HEAD]).astype(vt_ref.dtype)


def _shared_kv(x, norm_g, shift, scale, w_dl, w_dr, lat_g, w_k, w_v, kg_nope, kg_rope, cos, sin, tm=512):
    b, s, d = x.shape
    heads = w_k.shape[1] // QK_NOPE
    tile = lambda n: pl.BlockSpec((1, tm, n), lambda i, j: (i, j, 0))
    vec = pl.BlockSpec((1, 1, d), lambda i, j: (i, 0, 0))
    return pl.pallas_call(
        functools.partial(_kv_kernel, heads=heads),
        grid=(b, s // tm),
        in_specs=[tile(d), _const_spec((1, d)), vec, vec, _const_spec(w_dl.shape), _const_spec(w_dr.shape),
                  _const_spec(lat_g.shape), _const_spec(w_k.shape), _const_spec(w_v.shape),
                  _const_spec(kg_nope.shape), _const_spec(kg_rope.shape), tile(LANE), tile(LANE)],
        out_specs=[pl.BlockSpec((1, heads, tm, QK_PAD), lambda i, j: (i, 0, j, 0)),
                   pl.BlockSpec((1, heads, 1, V_HEAD, tm), lambda i, j: (i, 0, j, 0, 0))],
        out_shape=[jax.ShapeDtypeStruct((b, heads, s, QK_PAD), BF16),
                   jax.ShapeDtypeStruct((b, heads, s // tm, V_HEAD, tm), BF16)],
        compiler_params=_cparams("parallel", "parallel"),
        name="mla_kv",
    )(x, norm_g.reshape(1, d), shift, scale, w_dl, w_dr, lat_g, w_k, w_v, kg_nope, kg_rope, cos, sin)


def _q_kernel(x_ref, g_ref, sh_ref, sc_ref, wdq_ref, qlg_ref, wqn_ref, wqr_ref,
              qgn_ref, qgr_ref, cos_ref, sin_ref, q_ref, *, heads):
    hb = _modulated(x_ref[0], g_ref[...], sh_ref[0], sc_ref[0]).astype(BF16)
    ql = _dot(hb, wdq_ref[...])
    ql = ql * lax.rsqrt(jnp.mean(ql * ql, axis=-1, keepdims=True) + EPS) * qlg_ref[...]
    ql16 = ql.astype(BF16)
    q_nope = _dot(ql16, wqn_ref[...])
    q_rope = _dot(ql16, wqr_ref[...])
    sm_scale = (QK_NOPE + QK_ROPE) ** -0.5 * 1.4426950408889634
    for h in range(heads):
        sl = slice(h * LANE, (h + 1) * LANE)
        qn, qr = q_nope[:, sl], q_rope[:, sl]
        ssq = jnp.sum(qn * qn, axis=-1, keepdims=True) + jnp.sum(qr * qr, axis=-1, keepdims=True)
        inv = lax.rsqrt(ssq / (QK_NOPE + QK_ROPE) + EPS) * sm_scale
        q_ref[0, h, :, :QK_NOPE] = (qn * inv * qgn_ref[...]).astype(q_ref.dtype)
        q_ref[0, h, :, QK_NOPE:] = _rotate(qr * inv * qgr_ref[...], cos_ref[0], sin_ref[0]).astype(q_ref.dtype)


def _mla_q(x, norm_g, shift, scale, w_dq, ql_g, w_qn, w_qr, qg_nope, qg_rope, cos, sin, tm=512):
    b, s, d = x.shape
    heads = w_qn.shape[1] // QK_NOPE
    tile = lambda n: pl.BlockSpec((1, tm, n), lambda i, j: (i, j, 0))
    vec = pl.BlockSpec((1, 1, d), lambda i, j: (i, 0, 0))
    return pl.pallas_call(
        functools.partial(_q_kernel, heads=heads),
        grid=(b, s // tm),
        in_specs=[tile(d), _const_spec((1, d)), vec, vec, _const_spec(w_dq.shape), _const_spec(ql_g.shape),
                  _const_spec(w_qn.shape), _const_spec(w_qr.shape), _const_spec(qg_nope.shape),
                  _const_spec(qg_rope.shape), tile(LANE), tile(LANE)],
        out_specs=pl.BlockSpec((1, heads, tm, QK_PAD), lambda i, j: (i, 0, j, 0)),
        out_shape=jax.ShapeDtypeStruct((b, heads, s, QK_PAD), BF16),
        compiler_params=_cparams("parallel", "parallel"),
        name="mla_q",
    )(x, norm_g.reshape(1, d), shift, scale, w_dq, ql_g, w_qn, w_qr, qg_nope, qg_rope, cos, sin)


def _attn_kernel(q_ref, k_ref, vt_ref, o_ref, s_ref, m_ref, l_ref, acc_ref, *, tq):
    qi = pl.program_id(2)
    q = q_ref[0, 0]

    def scores(slot, j):
        kj = k_ref[0, 0, pl.ds(pl.multiple_of(j * tq, tq), tq), :]
        s_ref[slot] = _dot_nt(kj, q)

    def update(slot, j, masked):
        s = s_ref[slot]
        if masked:
            key_c = lax.broadcasted_iota(jnp.int32, (tq, tq), 0) // CHUNK
            qry_c = lax.broadcasted_iota(jnp.int32, (tq, tq), 1) // CHUNK
            s = jnp.where(key_c <= qry_c, s, -jnp.inf)
        m = m_ref[...]
        m_new = jnp.maximum(m, jnp.max(s, axis=0, keepdims=True))
        alpha = jnp.exp2(m - m_new)
        p = jnp.exp2(s - m_new)
        l_ref[...] = alpha * l_ref[...] + jnp.sum(p, axis=0, keepdims=True)
        acc_ref[...] = alpha * acc_ref[...] + _dot(vt_ref[0, 0, j], p.astype(BF16))
        m_ref[...] = m_new

    m_ref[...] = jnp.full(m_ref.shape, -jnp.inf, F32)
    l_ref[...] = jnp.zeros(l_ref.shape, F32)
    acc_ref[...] = jnp.zeros(acc_ref.shape, F32)
    scores(0, 0)

    @pl.loop(0, qi // 2)
    def _(t):
        scores(1, 2 * t + 1)
        update(0, 2 * t, False)
        scores(0, 2 * t + 2)
        update(1, 2 * t + 1, False)

    @pl.when(qi % 2 == 0)
    def _():
        update(0, qi, True)

    @pl.when(qi % 2 == 1)
    def _():
        scores(1, qi)
        update(0, qi - 1, False)
        update(1, qi, True)

    o_ref[0] = jnp.transpose(acc_ref[...] / l_ref[...]).astype(o_ref.dtype)


def _attention(q, k, vt):
    b, heads, s, dq = q.shape
    tq = vt.shape[-1]
    nkb = s // tq
    return pl.pallas_call(
        functools.partial(_attn_kernel, tq=tq),
        grid=(b, heads, s // tq),
        in_specs=[pl.BlockSpec((1, 1, tq, dq), lambda i, h, j: (i, h, j, 0)),
                  pl.BlockSpec((1, 1, s, dq), lambda i, h, j: (i, h, 0, 0)),
                  pl.BlockSpec((1, 1, nkb, V_HEAD, tq), lambda i, h, j: (i, h, 0, 0, 0))],
        out_specs=pl.BlockSpec((1, tq, V_HEAD), lambda i, h, j: (i, j, h)),
        out_shape=jax.ShapeDtypeStruct((b, s, heads * V_HEAD), BF16),
        scratch_shapes=[pltpu.VMEM((2, tq, tq), F32), pltpu.VMEM((1, tq), F32), pltpu.VMEM((1, tq), F32),
                        pltpu.VMEM((V_HEAD, tq), F32)],
        compiler_params=_cparams("parallel", "parallel", "parallel"),
        name="mla_attn",
    )(q, k, vt)


def _pad_cols(w, n):
    return jnp.pad(w, ((0, 0), (0, n - w.shape[1])))


def kernel(x, c, positions, ada_w, ada_b, norm_g, ffn_w_in, ffn_w_out, gdn_w_in, gdn_conv_w, gdn_a_log,
           gdn_dt_bias, gdn_norm_g, gdn_w_out, kv_ada_w, kv_ada_b, kv_norm_g, mla_w_dkv, mla_kv_norm_g,
           mla_w_ukv, mla_k_norm_g, mla_w_dq, mla_q_lora_norm_g, mla_w_uq, mla_q_norm_g, mla_w_out):
    b, s, d = x.shape
    depth = ada_w.shape[0]
    n_a = gdn_w_in.shape[0]
    n_mod = ada_w.shape[2] // d
    width = gdn_w_out.shape[1]
    g_heads = width // GDN_HEAD_DIM
    kv_lora = mla_kv_norm_g.shape[0]
    m_heads = mla_w_ukv.shape[1] // (QK_NOPE + V_HEAD)

    c_pad = jnp.pad(c, ((0, 8 - b), (0, 0)))
    mod = _modulation(c_pad, ada_w, ada_b)[:, :b].reshape(depth, b, n_mod, 1, d)
    kv_mod = _modulation(c_pad, kv_ada_w[None], kv_ada_b[None])[0, :b].reshape(b, 2, 1, d)
    cos, sin = _rope_tables(positions)

    w_in = ffn_w_in.astype(BF16)
    w_out = ffn_w_out.astype(BF16)
    gdn_w = gdn_w_in.astype(BF16)
    gdn_wo = gdn_w_out.astype(BF16)
    mla_wo = mla_w_out.astype(BF16)
    k_sh = vt_sh = None
    for l in range(depth):
        m = lambda i: mod[l, :, i]
        x = _ffn(x, norm_g[l, 0], m(0), m(1), m(2), w_in, w_out, (l, 0))
        if l < n_a:
            w = gdn_w_in[l]
            w_ba = jnp.concatenate([_pad_cols(w[:, 4 * width:4 * width + g_heads], LANE),
                                    _pad_cols(w[:, 4 * width + g_heads:], LANE)], axis=1).astype(BF16)
            qkv, z, ba = _gdn_in(x, norm_g[l, 1], m(3), m(4), gdn_w, l, width, w_ba)
            y = _gdn_core(qkv, z, ba, gdn_conv_w[l], gdn_a_log[l], gdn_dt_bias[l], gdn_norm_g[l])
            pre = (y, gdn_wo, l, m(5))
        else:
            j = l - n_a
            w_uq = mla_w_uq[j].reshape(-1, m_heads, QK_NOPE + QK_ROPE)
            w_qn = w_uq[:, :, :QK_NOPE].reshape(-1, m_heads * QK_NOPE).astype(BF16)
            w_qr = jnp.pad(w_uq[:, :, QK_NOPE:], ((0, 0), (0, 0), (0, LANE - QK_ROPE)))
            w_qr = w_qr.reshape(-1, m_heads * LANE).astype(BF16)
            qg = mla_q_norm_g[j]
            q = _mla_q(x, norm_g[l, 1], m(3), m(4), mla_w_dq[j].astype(BF16),
                       mla_q_lora_norm_g[j].reshape(1, -1), w_qn, w_qr,
                       qg[:QK_NOPE].reshape(1, -1), _pad_cols(qg[QK_NOPE:].reshape(1, -1), LANE), cos, sin)
            y = _attention(q, k_sh, vt_sh)
            pre = (y, mla_wo, j, m(5))
        x = _ffn(x, norm_g[l, 2], m(6), m(7), m(8), w_in, w_out, (l, 1), pre=pre)
        if l == n_a - 1:
            w_ukv = mla_w_ukv.reshape(kv_lora, m_heads, QK_NOPE + V_HEAD)
            w_k = w_ukv[:, :, :QK_NOPE].reshape(kv_lora, m_heads * QK_NOPE).astype(BF16)
            w_v = w_ukv[:, :, QK_NOPE:].reshape(kv_lora, m_heads * V_HEAD).astype(BF16)
            kg = mla_k_norm_g
            k_sh, vt_sh = _shared_kv(
                x, kv_norm_g, kv_mod[:, 0], kv_mod[:, 1], mla_w_dkv[:, :kv_lora].astype(BF16),
                _pad_cols(mla_w_dkv[:, kv_lora:], LANE).astype(BF16), mla_kv_norm_g.reshape(1, -1), w_k, w_v,
                kg[:QK_NOPE].reshape(1, -1), _pad_cols(kg[QK_NOPE:].reshape(1, -1), LANE), cos, sin)
    return x
```

```python
import functools

import jax
import jax.numpy as jnp
from jax import lax
from jax.experimental import pallas as pl
from jax.experimental.pallas import tpu as pltpu

F32 = jnp.float32
BF16 = jnp.bfloat16

EPS = 1e-6
CHUNK = 64
ROPE_BASE = 10000.0
LANE = 128
VMEM_LIMIT = 56 * 1024 * 1024

GDN_HEAD_DIM = 128
QK_NOPE = 128
QK_ROPE = 64
V_HEAD = 128
QK_PAD = 256


def _cparams(*sem):
    return pltpu.CompilerParams(dimension_semantics=sem, vmem_limit_bytes=VMEM_LIMIT)


def _silu(t):
    return t / (1.0 + jnp.exp(-t))


def _dot(a, b):
    return jnp.dot(a, b, preferred_element_type=F32)


def _dot_nt(a, b):
    return lax.dot_general(a, b, (((1,), (1,)), ((), ())), preferred_element_type=F32)


def _dot_tn(a, b):
    return lax.dot_general(a, b, (((0,), (0,)), ((), ())), preferred_element_type=F32)


def _split3(t):
    hi = t.astype(BF16)
    r1 = t - hi.astype(F32)
    mid = r1.astype(BF16)
    lo = (r1 - mid.astype(F32)).astype(BF16)
    return hi, mid, lo


def _const_spec(shape):
    nd = len(shape)
    return pl.BlockSpec(shape, lambda *_: (0,) * nd, pipeline_mode=pl.Buffered(1))


def _pick_spec(lead, block, tail=None):
    index = tuple(lead) + tuple(tail or (0,) * len(block))
    return pl.BlockSpec((None,) * len(lead) + tuple(block), lambda *_: index, pipeline_mode=pl.Buffered(1))


def _modulated(x, g, shift, scale):
    ms = jnp.mean(x * x, axis=-1, keepdims=True)
    return (x * lax.rsqrt(ms + EPS) * g) * (1.0 + scale) + shift


def _mod_kernel(c_ref, w_ref, b_ref, o_ref):
    ca = _silu(c_ref[...])
    c_hi = ca.astype(BF16)
    c_lo = (ca - c_hi.astype(F32)).astype(BF16)
    w = w_ref[0]
    w_hi = w.astype(BF16)
    w_lo = (w - w_hi.astype(F32)).astype(BF16)
    o_ref[0] = _dot(c_hi, w_hi) + (_dot(c_lo, w_hi) + _dot(c_hi, w_lo)) + b_ref[0]


def _modulation(c_pad, w, b):
    nl, d, n = w.shape
    tn = 1024
    return pl.pallas_call(
        _mod_kernel,
        grid=(nl, n // tn),
        in_specs=[
            pl.BlockSpec((8, d), lambda l, j: (0, 0)),
            pl.BlockSpec((1, d, tn), lambda l, j: (l, 0, j)),
            pl.BlockSpec((1, 1, tn), lambda l, j: (l, 0, j)),
        ],
        out_specs=pl.BlockSpec((1, 8, tn), lambda l, j: (l, 0, j)),
        out_shape=jax.ShapeDtypeStruct((nl, 8, n), F32),
        compiler_params=_cparams("parallel", "parallel"),
        name="adaln_mod",
    )(c_pad, w, b.reshape(nl, 1, n))


def _rope_kernel(pos_ref, cos_ref, sin_ref):
    half = QK_ROPE // 2
    pos = pos_ref[0].astype(F32)
    lane = lax.broadcasted_iota(jnp.int32, (1, LANE), 1)
    idx = jnp.where(lane < half, lane, lane - half).astype(F32)
    inv_freq = jnp.exp(idx * (-jnp.log(ROPE_BASE) / half))
    ang = pos * inv_freq
    valid = lane < QK_ROPE
    cos_ref[0] = jnp.where(valid, jnp.cos(ang), 0.0)
    sin_ref[0] = jnp.where(valid, jnp.where(lane < half, -jnp.sin(ang), jnp.sin(ang)), 0.0)


def _rope_tables(positions):
    b, s = positions.shape
    ts = 512
    out = jax.ShapeDtypeStruct((b, s, LANE), F32)
    return pl.pallas_call(
        _rope_kernel,
        grid=(b, s // ts),
        in_specs=[pl.BlockSpec((1, ts, 1), lambda i, j: (i, j, 0))],
        out_specs=[pl.BlockSpec((1, ts, LANE), lambda i, j: (i, j, 0))] * 2,
        out_shape=[out, out],
        compiler_params=_cparams("parallel", "parallel"),
        name="rope_tables",
    )(positions.reshape(b, s, 1))


def _rotate(t, cos, sin):
    half = QK_ROPE // 2
    lane = lax.broadcasted_iota(jnp.int32, t.shape, 1)
    swapped = jnp.where(lane < half, pltpu.roll(t, LANE - half, 1), pltpu.roll(t, half, 1))
    return t * cos + swapped * sin


def _ffn_kernel(*refs, pre):
    if pre:
        (x_ref, y_ref, wo_ref, gmix_ref, g_ref, sh_ref, sc_ref, gate_ref,
         wg_ref, wu_ref, wd_ref, out_ref) = refs
    else:
        x_ref, g_ref, sh_ref, sc_ref, gate_ref, wg_ref, wu_ref, wd_ref, out_ref = refs
    x = x_ref[0]
    if pre:
        x = x + gmix_ref[0] * _dot(y_ref[0], wo_ref[...])
    hb = _modulated(x, g_ref[...], sh_ref[0], sc_ref[0]).astype(BF16)
    gate = _dot(hb, wg_ref[...])
    up = _dot(hb, wu_ref[...])
    act = (_silu(gate) * up).astype(BF16)
    out_ref[0] = x + (0.5 * gate_ref[0]) * _dot(act, wd_ref[...])


def _ffn(x, norm_g, shift, scale, gate, w_in, w_out, idx, pre=None, tm=512):
    b, s, d = x.shape
    f = w_out.shape[-2]
    tile = pl.BlockSpec((1, tm, d), lambda i, j: (i, j, 0))
    vec = pl.BlockSpec((1, 1, d), lambda i, j: (i, 0, 0))
    args, specs = [x], [tile]
    if pre is not None:
        y, w_o, jo, g_mix = pre
        args += [y, w_o, g_mix]
        specs += [pl.BlockSpec((1, tm, y.shape[-1]), lambda i, j: (i, j, 0)), _pick_spec((jo,), w_o.shape[1:]), vec]
    args += [norm_g.reshape(1, d), shift, scale, gate, w_in, w_in, w_out]
    specs += [_const_spec((1, d)), vec, vec, vec,
              _pick_spec(idx, (d, f), (0, 0)), _pick_spec(idx, (d, f), (0, 1)), _pick_spec(idx, (f, d))]
    return pl.pallas_call(
        functools.partial(_ffn_kernel, pre=pre is not None),
        grid=(b, s // tm),
        in_specs=specs,
        out_specs=tile,
        out_shape=jax.ShapeDtypeStruct((b, s, d), F32),
        compiler_params=_cparams("parallel", "parallel"),
        name="ffn_pre" if pre is not None else "ffn",
    )(*args)


def _gdn_in_kernel(x_ref, g_ref, sh_ref, sc_ref, w_ref, wba_ref, qkv_ref, z_ref, ba_ref):
    hb = _modulated(x_ref[0], g_ref[...], sh_ref[0], sc_ref[0]).astype(BF16)
    width = z_ref.shape[-1]
    p = _dot(hb, w_ref[...])
    qkv_ref[0] = p[:, :3 * width]
    z_ref[0] = p[:, 3 * width:]
    ba_ref[0] = _dot(hb, wba_ref[...])


def _gdn_in(x, norm_g, shift, scale, w_all, l, width, w_ba, tm=512):
    b, s, d = x.shape
    tile = lambda n: pl.BlockSpec((1, tm, n), lambda i, j: (i, j, 0))
    vec = pl.BlockSpec((1, 1, d), lambda i, j: (i, 0, 0))
    return pl.pallas_call(
        _gdn_in_kernel,
        grid=(b, s // tm),
        in_specs=[tile(d), _const_spec((1, d)), vec, vec, _pick_spec((l,), (d, 4 * width)), _const_spec(w_ba.shape)],
        out_specs=[tile(3 * width), tile(width), tile(2 * LANE)],
        out_shape=[jax.ShapeDtypeStruct((b, s, 3 * width), F32),
                   jax.ShapeDtypeStruct((b, s, width), F32),
                   jax.ShapeDtypeStruct((b, s, 2 * LANE), F32)],
        compiler_params=_cparams("parallel", "parallel"),
        name="gdn_in",
    )(x, norm_g.reshape(1, d), shift, scale, w_all, w_ba)


def _split2(t):
    hi = t.astype(BF16)
    return hi, (t - hi.astype(F32)).astype(BF16)


def _dot2(pieces, m):
    return _dot(pieces[0], m) + _dot(pieces[1], m)


def _dot2_left(m, pieces):
    return _dot(m, pieces[0]) + _dot(m, pieces[1])


def _tile_rows(t, n):
    return jnp.concatenate([t] * n, axis=0)


def _iota2(shape):
    return lax.broadcasted_iota(jnp.int32, shape, 0), lax.broadcasted_iota(jnp.int32, shape, 1)


GDN_GROUP = 4


def _gdn_intra_kernel(qkv_ref, ba_ref, convw_ref, alog_ref, dtb_ref,
                      u_ref, w_ref, qe_ref, kd_ref, attn_ref, egl_ref, ext_ref, *, heads, nchunk):
    c = CHUNK
    dk = GDN_HEAD_DIM
    width = heads * dk
    ts = nchunk * c
    gw = GDN_GROUP * dk
    pw_ = GDN_GROUP * c
    ngroup = heads // GDN_GROUP

    @pl.when(pl.program_id(1) == 0)
    def _():
        ext_ref[0:8, :] = jnp.zeros((8, 3 * width), F32)

    cur = qkv_ref[0]
    ext_ref[8:8 + ts, :] = cur
    k_taps = convw_ref.shape[0]
    acc = ext_ref[8:8 + ts, :] * convw_ref[k_taps - 1:k_taps, :]
    for j in range(k_taps - 1):
        off = 8 - (k_taps - 1) + j
        acc = acc + ext_ref[off:off + ts, :] * convw_ref[j:j + 1, :]
    ext_ref[0:8, :] = cur[ts - 8:ts, :]
    act = _silu(acc)

    def l2n(t, scale):
        parts = []
        for h in range(heads):
            th = t[:, h * dk:(h + 1) * dk]
            parts.append(th * (lax.rsqrt(jnp.sum(th * th, axis=-1, keepdims=True) + EPS) * scale))
        return jnp.concatenate(parts, axis=1)

    qn = l2n(act[:, :width], dk ** -0.5)
    kn = l2n(act[:, width:2 * width], 1.0)
    v16 = act[:, 2 * width:].astype(BF16)
    k16 = kn.astype(BF16)
    q16 = qn.astype(BF16)

    ba = ba_ref[0]
    beta = 1.0 / (1.0 + jnp.exp(-ba[:, :LANE]))
    al = ba[:, LANE:] + dtb_ref[...]
    softplus = jnp.maximum(al, 0.0) + jnp.log1p(jnp.exp(-jnp.abs(al)))
    lane = lax.broadcasted_iota(jnp.int32, (ts, LANE), 1)
    gl = jnp.where(lane < heads, -jnp.exp(alog_ref[...]) * softplus, 0.0)

    r, cc = _iota2((ts, ts))
    tri = ((r // c == cc // c) & (r >= cc)).astype(BF16)
    gcum = _dot2_left(tri, _split2(gl))
    r, cc = _iota2((LANE, width))
    e_full = (r == cc // dk).astype(BF16)
    r, cc = _iota2((LANE, heads * c))
    e_pack = (r == cc // c).astype(BF16)
    gc_pieces = _split2(gcum)
    gc_full = _dot2(gc_pieces, e_full)
    gc_pack = _dot2(gc_pieces, e_pack)
    beta_pack = _dot(beta.astype(BF16), e_pack)

    qe_ref[0] = (qn * jnp.exp(gc_full)).astype(qe_ref.dtype)

    r, cc = _iota2((c, pw_))
    eye_p = r == cc % c
    incl_p = r >= cc % c
    strict_p = r > cc % c
    r, cc = _iota2((pw_, gw))
    bd_wide = r // c == cc // dk
    r, cc = _iota2((pw_, pw_))
    bd_sq = r // c == cc // c
    zero16 = jnp.zeros((), BF16)

    units = [(ci, g) for ci in range(nchunk) for g in range(ngroup)]
    rows = lambda ci: slice(ci * c, (ci + 1) * c)
    gsl = lambda g: slice(g * gw, (g + 1) * gw)
    psl = lambda g: slice(g * pw_, (g + 1) * pw_)

    for ci in range(nchunk):
        g_last = gc_full[ci * c + c - 1:ci * c + c, :]
        egl_ref[0, ci] = jnp.exp(g_last)
        kd_ref[0, rows(ci), :] = (kn[rows(ci)] * jnp.exp(g_last - gc_full[rows(ci)])).astype(kd_ref.dtype)

    bd_k = [jnp.where(bd_wide, _tile_rows(k16[rows(ci), gsl(g)], GDN_GROUP), zero16) for ci, g in units]
    sc = [_dot_nt(jnp.concatenate([k16[rows(ci), gsl(g)], q16[rows(ci), gsl(g)]], axis=0), bd_k[i])
          for i, (ci, g) in enumerate(units)]
    a_p, beta_r, eg_r = [], [], []
    for i, (ci, g) in enumerate(units):
        gcp = gc_pack[rows(ci), psl(g)]
        gr = jnp.sum(jnp.where(eye_p, gcp, 0.0), axis=0, keepdims=True)
        bp = beta_pack[rows(ci), psl(g)]
        beta_r.append(jnp.sum(jnp.where(eye_p, bp, 0.0), axis=0, keepdims=True))
        eg_r.append(jnp.exp(gr))
        decay = jnp.where(incl_p, jnp.exp(jnp.where(incl_p, gcp - gr, 0.0)), 0.0)
        a_p.append(jnp.where(strict_p, sc[i][:c] * bp * decay, 0.0))
        attn_ref[0, rows(ci), psl(g)] = jnp.where(incl_p, sc[i][c:] * decay, 0.0).astype(attn_ref.dtype)

    def bd(t16):
        return jnp.where(bd_sq, _tile_rows(t16, GDN_GROUP), zero16)

    t_inv = [jnp.where(eye_p, 1.0, 0.0) - a for a in a_p]
    a16 = [a.astype(BF16) for a in a_p]
    pw = [_dot(a16[i], bd(a16[i])) for i in range(len(units))]
    n_sq = c.bit_length() - 2
    for step in range(n_sq):
        pw16 = [p.astype(BF16) for p in pw]
        if step < n_sq - 1:
            prod = [_dot(jnp.concatenate([t_inv[i].astype(BF16), pw16[i]], axis=0), bd(pw16[i]))
                    for i in range(len(units))]
            t_inv = [t_inv[i] + prod[i][:c] for i in range(len(units))]
            pw = [prod[i][c:] for i in range(len(units))]
        else:
            t_inv = [t_inv[i] + _dot(t_inv[i].astype(BF16), bd(pw16[i])) for i in range(len(units))]

    for i, (ci, g) in enumerate(units):
        t_u = (t_inv[i] * beta_r[i]).astype(BF16)
        t_w = (t_inv[i] * (beta_r[i] * eg_r[i])).astype(BF16)
        bd_v = jnp.where(bd_wide, _tile_rows(v16[rows(ci), gsl(g)], GDN_GROUP), zero16)
        u_ref[0, rows(ci), gsl(g)] = _dot(t_u, bd_v).astype(u_ref.dtype)
        w_ref[0, rows(ci), gsl(g)] = _dot(t_w, bd_k[i]).astype(w_ref.dtype)


def _gdn_seq_kernel(u_ref, w_ref, qe_ref, kd_ref, attn_ref, egl_ref, z_ref, ng_ref, o_ref, state_ref,
                    *, heads, nchunk):
    c = CHUNK
    dk = GDN_HEAD_DIM
    gw = GDN_GROUP * dk
    pw_ = GDN_GROUP * c
    ngroup = heads // GDN_GROUP

    @pl.when(pl.program_id(1) == 0)
    def _():
        state_ref[...] = jnp.zeros_like(state_ref)

    r, cc = _iota2((pw_, gw))
    bd_wide = r // c == cc // dk
    zero16 = jnp.zeros((), BF16)
    hsl = lambda h: slice(h * dk, (h + 1) * dk)
    for ci in range(nchunk):
        rows = slice(ci * c, (ci + 1) * c)
        st16 = [state_ref[h].astype(BF16) for h in range(heads)]
        rd = [_dot(jnp.concatenate([w_ref[0, rows, hsl(h)], qe_ref[0, rows, hsl(h)]], axis=0), st16[h])
              for h in range(heads)]
        vn16 = [(u_ref[0, rows, hsl(h)] - rd[h][:c]).astype(BF16) for h in range(heads)]
        intra = []
        for g in range(ngroup):
            vg = jnp.concatenate(vn16[g * GDN_GROUP:(g + 1) * GDN_GROUP], axis=1)
            bd_vn = jnp.where(bd_wide, _tile_rows(vg, GDN_GROUP), zero16)
            intra.append(_dot(attn_ref[0, rows, g * pw_:(g + 1) * pw_], bd_vn))
        for h in range(heads):
            state_ref[h] = state_ref[h] * egl_ref[0, ci, :, hsl(h)] + _dot_tn(kd_ref[0, rows, hsl(h)], vn16[h])
        for h in range(heads):
            g, hh = divmod(h, GDN_GROUP)
            o = rd[h][c:] + intra[g][:, hh * dk:(hh + 1) * dk]
            o = o * lax.rsqrt(jnp.mean(o * o, axis=-1, keepdims=True) + EPS) * ng_ref[...]
            o_ref[0, rows, hsl(h)] = (o * _silu(z_ref[0, rows, hsl(h)])).astype(o_ref.dtype)


def _gdn_core(qkv, z, ba, conv_w, a_log, dt_bias, norm_g, nchunk=4):
    b, s, width = z.shape
    heads = width // GDN_HEAD_DIM
    ts = nchunk * CHUNK
    pad = lambda t: jnp.pad(t.reshape(1, heads), ((0, 0), (0, LANE - heads)))
    tile = lambda n: pl.BlockSpec((1, ts, n), lambda i, j: (i, j, 0))
    egl_spec = pl.BlockSpec((1, nchunk, 1, width), lambda i, j: (i, j, 0, 0))
    packed = heads * CHUNK
    u, w, qe, kd, attn, egl = pl.pallas_call(
        functools.partial(_gdn_intra_kernel, heads=heads, nchunk=nchunk),
        grid=(b, s // ts),
        in_specs=[tile(3 * width), tile(2 * LANE), _const_spec(conv_w.shape),
                  _const_spec((1, LANE)), _const_spec((1, LANE))],
        out_specs=[tile(width), tile(width), tile(width), tile(width), tile(packed), egl_spec],
        out_shape=[jax.ShapeDtypeStruct((b, s, width), F32),
                   jax.ShapeDtypeStruct((b, s, width), BF16),
                   jax.ShapeDtypeStruct((b, s, width), BF16),
                   jax.ShapeDtypeStruct((b, s, width), BF16),
                   jax.ShapeDtypeStruct((b, s, packed), BF16),
                   jax.ShapeDtypeStruct((b, s // CHUNK, 1, width), F32)],
        scratch_shapes=[pltpu.VMEM((8 + ts, 3 * width), F32)],
        compiler_params=_cparams("parallel", "arbitrary"),
        name="gdn_intra",
    )(qkv, ba, conv_w, pad(a_log), pad(dt_bias))
    return pl.pallas_call(
        functools.partial(_gdn_seq_kernel, heads=heads, nchunk=nchunk),
        grid=(b, s // ts),
        in_specs=[tile(width), tile(width), tile(width), tile(width), tile(packed), egl_spec,
                  tile(width), _const_spec((1, GDN_HEAD_DIM))],
        out_specs=tile(width),
        out_shape=jax.ShapeDtypeStruct((b, s, width), BF16),
        scratch_shapes=[pltpu.VMEM((heads, GDN_HEAD_DIM, GDN_HEAD_DIM), F32)],
        compiler_params=_cparams("parallel", "arbitrary"),
        name="gdn_seq",
    )(u, w, qe, kd, attn, egl, z, norm_g.reshape(1, GDN_HEAD_DIM))


def _kv_kernel(x_ref, g_ref, sh_ref, sc_ref, wdl_ref, wdr_ref, lg_ref, wk_ref, wv_ref,
               kgn_ref, kgr_ref, cos_ref, sin_ref, k_ref, vt_ref, *, heads):
    hb = _modulated(x_ref[0], g_ref[...], sh_ref[0], sc_ref[0]).astype(BF16)
    lat = _dot(hb, wdl_ref[...])
    rope = _dot(hb, wdr_ref[...])
    lat = lat * lax.rsqrt(jnp.mean(lat * lat, axis=-1, keepdims=True) + EPS) * lg_ref[...]
    lat16 = lat.astype(BF16)
    k_nope = _dot(lat16, wk_ref[...])
    v = _dot(lat16, wv_ref[...])
    rope_sq = jnp.sum(rope * rope, axis=-1, keepdims=True)
    for h in range(heads):
        sl = slice(h * QK_NOPE, (h + 1) * QK_NOPE)
        kn = k_nope[:, sl]
        inv = lax.rsqrt((jnp.sum(kn * kn, axis=-1, keepdims=True) + rope_sq) / (QK_NOPE + QK_ROPE) + EPS)
        k_ref[0, h, :, :QK_NOPE] = (kn * inv * kgn_ref[...]).astype(k_ref.dtype)
        kr = _rotate(rope * inv * kgr_ref[...], cos_ref[0], sin_ref[0])
        k_ref[0, h, :, QK_NOPE:] = kr.astype(k_ref.dtype)
        vt_ref[0, h, 0] = jnp.transpose(v[:, h * V_HEAD:(h + 1) * V_HEAD]).astype(vt_ref.dtype)


def _shared_kv(x, norm_g, shift, scale, w_dl, w_dr, lat_g, w_k, w_v, kg_nope, kg_rope, cos, sin, tm=512):
    b, s, d = x.shape
    heads = w_k.shape[1] // QK_NOPE
    tile = lambda n: pl.BlockSpec((1, tm, n), lambda i, j: (i, j, 0))
    vec = pl.BlockSpec((1, 1, d), lambda i, j: (i, 0, 0))
    return pl.pallas_call(
        functools.partial(_kv_kernel, heads=heads),
        grid=(b, s // tm),
        in_specs=[tile(d), _const_spec((1, d)), vec, vec, _const_spec(w_dl.shape), _const_spec(w_dr.shape),
                  _const_spec(lat_g.shape), _const_spec(w_k.shape), _const_spec(w_v.shape),
                  _const_spec(kg_nope.shape), _const_spec(kg_rope.shape), tile(LANE), tile(LANE)],
        out_specs=[pl.BlockSpec((1, heads, tm, QK_PAD), lambda i, j: (i, 0, j, 0)),
                   pl.BlockSpec((1, heads, 1, V_HEAD, tm), lambda i, j: (i, 0, j, 0, 0))],
        out_shape=[jax.ShapeDtypeStruct((b, heads, s, QK_PAD), BF16),
                   jax.ShapeDtypeStruct((b, heads, s // tm, V_HEAD, tm), BF16)],
        compiler_params=_cparams("parallel", "parallel"),
        name="mla_kv",
    )(x, norm_g.reshape(1, d), shift, scale, w_dl, w_dr, lat_g, w_k, w_v, kg_nope, kg_rope, cos, sin)


def _q_kernel(x_ref, g_ref, sh_ref, sc_ref, wdq_ref, qlg_ref, wqn_ref, wqr_ref,
              qgn_ref, qgr_ref, cos_ref, sin_ref, q_ref, *, heads):
    hb = _modulated(x_ref[0], g_ref[...], sh_ref[0], sc_ref[0]).astype(BF16)
    ql = _dot(hb, wdq_ref[...])
    ql = ql * lax.rsqrt(jnp.mean(ql * ql, axis=-1, keepdims=True) + EPS) * qlg_ref[...]
    ql16 = ql.astype(BF16)
    q_nope = _dot(ql16, wqn_ref[...])
    q_rope = _dot(ql16, wqr_ref[...])
    sm_scale = (QK_NOPE + QK_ROPE) ** -0.5 * 1.4426950408889634
    for h in range(heads):
        sl = slice(h * LANE, (h + 1) * LANE)
        qn, qr = q_nope[:, sl], q_rope[:, sl]
        ssq = jnp.sum(qn * qn, axis=-1, keepdims=True) + jnp.sum(qr * qr, axis=-1, keepdims=True)
        inv = lax.rsqrt(ssq / (QK_NOPE + QK_ROPE) + EPS) * sm_scale
        q_ref[0, h, :, :QK_NOPE] = (qn * inv * qgn_ref[...]).astype(q_ref.dtype)
        q_ref[0, h, :, QK_NOPE:] = _rotate(qr * inv * qgr_ref[...], cos_ref[0], sin_ref[0]).astype(q_ref.dtype)


def _mla_q(x, norm_g, shift, scale, w_dq, ql_g, w_qn, w_qr, qg_nope, qg_rope, cos, sin, tm=512):
    b, s, d = x.shape
    heads = w_qn.shape[1] // QK_NOPE
    tile = lambda n: pl.BlockSpec((1, tm, n), lambda i, j: (i, j, 0))
    vec = pl.BlockSpec((1, 1, d), lambda i, j: (i, 0, 0))
    return pl.pallas_call(
        functools.partial(_q_kernel, heads=heads),
        grid=(b, s // tm),
        in_specs=[tile(d), _const_spec((1, d)), vec, vec, _const_spec(w_dq.shape), _const_spec(ql_g.shape),
                  _const_spec(w_qn.shape), _const_spec(w_qr.shape), _const_spec(qg_nope.shape),
                  _const_spec(qg_rope.shape), tile(LANE), tile(LANE)],
        out_specs=pl.BlockSpec((1, heads, tm, QK_PAD), lambda i, j: (i, 0, j, 0)),
        out_shape=jax.ShapeDtypeStruct((b, heads, s, QK_PAD), BF16),
        compiler_params=_cparams("parallel", "parallel"),
        name="mla_q",
    )(x, norm_g.reshape(1, d), shift, scale, w_dq, ql_g, w_qn, w_qr, qg_nope, qg_rope, cos, sin)


def _attn_kernel(q_ref, k_ref, vt_ref, o_ref, s_ref, m_ref, l_ref, acc_ref, *, tq):
    nq = q_ref.shape[2] // tq
    pairs = [(qi, j) for qi in range(nq) for j in range(qi + 1)]

    def scores(slot, qi, j):
        s_ref[slot] = _dot_nt(k_ref[0, 0, j * tq:(j + 1) * tq, :], q_ref[0, 0, qi * tq:(qi + 1) * tq, :])

    def update(slot, qi, j):
        s = s_ref[slot]
        if j == qi:
            key_c = lax.broadcasted_iota(jnp.int32, (tq, tq), 0) // CHUNK
            qry_c = lax.broadcasted_iota(jnp.int32, (tq, tq), 1) // CHUNK
            s = jnp.where(key_c <= qry_c, s, -jnp.inf)
        s_max = jnp.max(s, axis=0, keepdims=True)
        if j == 0:
            m_new = s_max
            p = jnp.exp2(s - m_new)
            l = jnp.sum(p, axis=0, keepdims=True)
            acc = _dot(vt_ref[0, 0, j], p.astype(BF16))
        else:
            m = m_ref[...]
            m_new = jnp.maximum(m, s_max)
            alpha = jnp.exp2(m - m_new)
            p = jnp.exp2(s - m_new)
            l = alpha * l_ref[...] + jnp.sum(p, axis=0, keepdims=True)
            acc = alpha * acc_ref[...] + _dot(vt_ref[0, 0, j], p.astype(BF16))
        if j == qi:
            o_ref[0, qi * tq:(qi + 1) * tq, :] = jnp.transpose(acc / l).astype(o_ref.dtype)
        else:
            m_ref[...] = m_new
            l_ref[...] = l
            acc_ref[...] = acc

    scores(0, *pairs[0])
    for t, (qi, j) in enumerate(pairs):
        if t + 1 < len(pairs):
            scores((t + 1) % 2, *pairs[t + 1])
        update(t % 2, qi, j)


def _attention(q, k, vt):
    b, heads, s, dq = q.shape
    tq = vt.shape[-1]
    nkb = s // tq
    whole = lambda i, h: (i, h, 0, 0)
    return pl.pallas_call(
        functools.partial(_attn_kernel, tq=tq),
        grid=(b, heads),
        in_specs=[pl.BlockSpec((1, 1, s, dq), whole),
                  pl.BlockSpec((1, 1, s, dq), whole),
                  pl.BlockSpec((1, 1, nkb, V_HEAD, tq), lambda i, h: (i, h, 0, 0, 0))],
        out_specs=pl.BlockSpec((1, s, V_HEAD), lambda i, h: (i, 0, h)),
        out_shape=jax.ShapeDtypeStruct((b, s, heads * V_HEAD), BF16),
        scratch_shapes=[pltpu.VMEM((2, tq, tq), F32), pltpu.VMEM((1, tq), F32), pltpu.VMEM((1, tq), F32),
                        pltpu.VMEM((V_HEAD, tq), F32)],
        compiler_params=_cparams("parallel", "parallel"),
        name="mla_attn",
    )(q, k, vt)


def _pad_cols(w, n):
    return jnp.pad(w, ((0, 0), (0, n - w.shape[1])))


def kernel(x, c, positions, ada_w, ada_b, norm_g, ffn_w_in, ffn_w_out, gdn_w_in, gdn_conv_w, gdn_a_log,
           gdn_dt_bias, gdn_norm_g, gdn_w_out, kv_ada_w, kv_ada_b, kv_norm_g, mla_w_dkv, mla_kv_norm_g,
           mla_w_ukv, mla_k_norm_g, mla_w_dq, mla_q_lora_norm_g, mla_w_uq, mla_q_norm_g, mla_w_out):
    b, s, d = x.shape
    depth = ada_w.shape[0]
    n_a = gdn_w_in.shape[0]
    n_mod = ada_w.shape[2] // d
    width = gdn_w_out.shape[1]
    g_heads = width // GDN_HEAD_DIM
    kv_lora = mla_kv_norm_g.shape[0]
    m_heads = mla_w_ukv.shape[1] // (QK_NOPE + V_HEAD)

    c_pad = jnp.pad(c, ((0, 8 - b), (0, 0)))
    mod = _modulation(c_pad, ada_w, ada_b)[:, :b].reshape(depth, b, n_mod, 1, d)
    kv_mod = _modulation(c_pad, kv_ada_w[None], kv_ada_b[None])[0, :b].reshape(b, 2, 1, d)
    cos, sin = _rope_tables(positions)

    w_in = ffn_w_in.astype(BF16)
    w_out = ffn_w_out.astype(BF16)
    gdn_w = gdn_w_in.astype(BF16)
    gdn_wo = gdn_w_out.astype(BF16)
    mla_wo = mla_w_out.astype(BF16)
    k_sh = vt_sh = None
    for l in range(depth):
        m = lambda i: mod[l, :, i]
        x = _ffn(x, norm_g[l, 0], m(0), m(1), m(2), w_in, w_out, (l, 0))
        if l < n_a:
            w = gdn_w_in[l]
            w_ba = jnp.concatenate([_pad_cols(w[:, 4 * width:4 * width + g_heads], LANE),
                                    _pad_cols(w[:, 4 * width + g_heads:], LANE)], axis=1).astype(BF16)
            qkv, z, ba = _gdn_in(x, norm_g[l, 1], m(3), m(4), gdn_w, l, width, w_ba)
            y = _gdn_core(qkv, z, ba, gdn_conv_w[l], gdn_a_log[l], gdn_dt_bias[l], gdn_norm_g[l])
            pre = (y, gdn_wo, l, m(5))
        else:
            j = l - n_a
            w_uq = mla_w_uq[j].reshape(-1, m_heads, QK_NOPE + QK_ROPE)
            w_qn = w_uq[:, :, :QK_NOPE].reshape(-1, m_heads * QK_NOPE).astype(BF16)
            w_qr = jnp.pad(w_uq[:, :, QK_NOPE:], ((0, 0), (0, 0), (0, LANE - QK_ROPE)))
            w_qr = w_qr.reshape(-1, m_heads * LANE).astype(BF16)
            qg = mla_q_norm_g[j]
            q = _mla_q(x, norm_g[l, 1], m(3), m(4), mla_w_dq[j].astype(BF16),
                       mla_q_lora_norm_g[j].reshape(1, -1), w_qn, w_qr,
                       qg[:QK_NOPE].reshape(1, -1), _pad_cols(qg[QK_NOPE:].reshape(1, -1), LANE), cos, sin)
            y = _attention(q, k_sh, vt_sh)
            pre = (y, mla_wo, j, m(5))
        x = _ffn(x, norm_g[l, 2], m(6), m(7), m(8), w_in, w_out, (l, 1), pre=pre)
        if l == n_a - 1:
            w_ukv = mla_w_ukv.reshape(kv_lora, m_heads, QK_NOPE + V_HEAD)
            w_k = w_ukv[:, :, :QK_NOPE].reshape(kv_lora, m_heads * QK_NOPE).astype(BF16)
            w_v = w_ukv[:, :, QK_NOPE:].reshape(kv_lora, m_heads * V_HEAD).astype(BF16)
            kg = mla_k_norm_g
            k_sh, vt_sh = _shared_kv(
                x, kv_norm_g, kv_mod[:, 0], kv_mod[:, 1], mla_w_dkv[:, :kv_lora].astype(BF16),
                _pad_cols(mla_w_dkv[:, kv_lora:], LANE).astype(BF16), mla_kv_norm_g.reshape(1, -1), w_k, w_v,
                kg[:QK_NOPE].reshape(1, -1), _pad_cols(kg[QK_NOPE:].reshape(1, -1), LANE), cos, sin)
    return x
```

```python
import functools

import jax
import jax.numpy as jnp
from jax import lax
from jax.experimental import pallas as pl
from jax.experimental.pallas import tpu as pltpu

F32 = jnp.float32
BF16 = jnp.bfloat16

EPS = 1e-6
CHUNK = 64
ROPE_BASE = 10000.0
LANE = 128
VMEM_LIMIT = 56 * 1024 * 1024

GDN_HEAD_DIM = 128
QK_NOPE = 128
QK_ROPE = 64
V_HEAD = 128
V_AUG = V_HEAD + 16
QK_PAD = 256


def _cparams(*sem):
    return pltpu.CompilerParams(dimension_semantics=sem, vmem_limit_bytes=VMEM_LIMIT)


def _silu(t):
    half = 0.5 * t
    return half + half * jnp.tanh(half)


def _dot(a, b):
    return jnp.dot(a, b, preferred_element_type=F32)


def _dot_nt(a, b):
    return lax.dot_general(a, b, (((1,), (1,)), ((), ())), preferred_element_type=F32)


def _dot_tn(a, b):
    return lax.dot_general(a, b, (((0,), (0,)), ((), ())), preferred_element_type=F32)


def _split3(t):
    hi = t.astype(BF16)
    r1 = t - hi.astype(F32)
    mid = r1.astype(BF16)
    lo = (r1 - mid.astype(F32)).astype(BF16)
    return hi, mid, lo


def _const_spec(shape):
    nd = len(shape)
    return pl.BlockSpec(shape, lambda *_: (0,) * nd, pipeline_mode=pl.Buffered(1))


def _pick_spec(lead, block, tail=None):
    index = tuple(lead) + tuple(tail or (0,) * len(block))
    return pl.BlockSpec((None,) * len(lead) + tuple(block), lambda *_: index, pipeline_mode=pl.Buffered(1))


def _modulated(x, g, shift, scale):
    ms = jnp.mean(x * x, axis=-1, keepdims=True)
    return (x * lax.rsqrt(ms + EPS)) * (g * (1.0 + scale)) + shift


def _mod_kernel(c_ref, w_ref, b_ref, o_ref):
    ca = _silu(c_ref[...])
    c_hi = ca.astype(BF16)
    c_lo = (ca - c_hi.astype(F32)).astype(BF16)
    w = w_ref[0]
    w_hi = w.astype(BF16)
    w_lo = (w - w_hi.astype(F32)).astype(BF16)
    o_ref[0] = _dot(c_hi, w_hi) + (_dot(c_lo, w_hi) + _dot(c_hi, w_lo)) + b_ref[0]


def _modulation(c_pad, w, b):
    nl, d, n = w.shape
    tn = 1024
    return pl.pallas_call(
        _mod_kernel,
        grid=(nl, n // tn),
        in_specs=[
            pl.BlockSpec((8, d), lambda l, j: (0, 0)),
            pl.BlockSpec((1, d, tn), lambda l, j: (l, 0, j)),
            pl.BlockSpec((1, 1, tn), lambda l, j: (l, 0, j)),
        ],
        out_specs=pl.BlockSpec((1, 8, tn), lambda l, j: (l, 0, j)),
        out_shape=jax.ShapeDtypeStruct((nl, 8, n), F32),
        compiler_params=_cparams("parallel", "parallel"),
        name="adaln_mod",
    )(c_pad, w, b.reshape(nl, 1, n))


def _rope_kernel(pos_ref, cos_ref, sin_ref):
    half = QK_ROPE // 2
    pos = pos_ref[0].astype(F32)
    lane = lax.broadcasted_iota(jnp.int32, (1, LANE), 1)
    idx = jnp.where(lane < half, lane, lane - half).astype(F32)
    inv_freq = jnp.exp(idx * (-jnp.log(ROPE_BASE) / half))
    ang = pos * inv_freq
    valid = lane < QK_ROPE
    cos_ref[0] = jnp.where(valid, jnp.cos(ang), 0.0)
    sin_ref[0] = jnp.where(valid, jnp.where(lane < half, -jnp.sin(ang), jnp.sin(ang)), 0.0)


def _rope_tables(positions):
    b, s = positions.shape
    ts = 512
    out = jax.ShapeDtypeStruct((b, s, LANE), F32)
    return pl.pallas_call(
        _rope_kernel,
        grid=(b, s // ts),
        in_specs=[pl.BlockSpec((1, ts, 1), lambda i, j: (i, j, 0))],
        out_specs=[pl.BlockSpec((1, ts, LANE), lambda i, j: (i, j, 0))] * 2,
        out_shape=[out, out],
        compiler_params=_cparams("parallel", "parallel"),
        name="rope_tables",
    )(positions.reshape(b, s, 1))


def _rotate(t, cos, sin):
    half = QK_ROPE // 2
    lane = lax.broadcasted_iota(jnp.int32, t.shape, 1)
    swapped = jnp.where(lane < half, pltpu.roll(t, LANE - half, 1), pltpu.roll(t, half, 1))
    return t * cos + swapped * sin


def _ffn_kernel(*refs, pre):
    if pre:
        (x_ref, y_ref, wo_ref, gmix_ref, g_ref, sh_ref, sc_ref, gate_ref,
         wg_ref, wu_ref, wd_ref, out_ref) = refs
    else:
        x_ref, g_ref, sh_ref, sc_ref, gate_ref, wg_ref, wu_ref, wd_ref, out_ref = refs
    x = x_ref[0]
    if pre:
        x = x + gmix_ref[0] * _dot(y_ref[0], wo_ref[...])
    hb = _modulated(x, g_ref[...], sh_ref[0], sc_ref[0]).astype(BF16)
    gate = _dot(hb, wg_ref[...])
    up = _dot(hb, wu_ref[...])
    act = (_silu(gate) * up).astype(BF16)
    out_ref[0] = x + (0.5 * gate_ref[0]) * _dot(act, wd_ref[...])


def _ffn(x, norm_g, shift, scale, gate, w_in, w_out, idx, pre=None, tm=512):
    b, s, d = x.shape
    f = w_out.shape[-2]
    tile = pl.BlockSpec((1, tm, d), lambda i, j: (i, j, 0))
    vec = pl.BlockSpec((1, 1, d), lambda i, j: (i, 0, 0))
    args, specs = [x], [tile]
    if pre is not None:
        y, w_o, jo, g_mix = pre
        args += [y, w_o, g_mix]
        specs += [pl.BlockSpec((1, tm, y.shape[-1]), lambda i, j: (i, j, 0)), _pick_spec((jo,), w_o.shape[1:]), vec]
    args += [norm_g.reshape(1, d), shift, scale, gate, w_in, w_in, w_out]
    specs += [_const_spec((1, d)), vec, vec, vec,
              _pick_spec(idx, (d, f), (0, 0)), _pick_spec(idx, (d, f), (0, 1)), _pick_spec(idx, (f, d))]
    return pl.pallas_call(
        functools.partial(_ffn_kernel, pre=pre is not None),
        grid=(b, s // tm),
        in_specs=specs,
        out_specs=tile,
        out_shape=jax.ShapeDtypeStruct((b, s, d), F32),
        compiler_params=_cparams("parallel", "parallel"),
        name="ffn_pre" if pre is not None else "ffn",
    )(*args)


def _gdn_in_kernel(x_ref, g_ref, sh_ref, sc_ref, w_ref, wba_ref, qkv_ref, z_ref, ba_ref):
    hb = _modulated(x_ref[0], g_ref[...], sh_ref[0], sc_ref[0]).astype(BF16)
    width = z_ref.shape[-1]
    p = _dot(hb, w_ref[...])
    qkv_ref[0] = p[:, :3 * width]
    z_ref[0] = p[:, 3 * width:]
    ba_ref[0] = _dot(hb, wba_ref[...])


def _gdn_in(x, norm_g, shift, scale, w_all, l, width, w_ba, tm=512):
    b, s, d = x.shape
    tile = lambda n: pl.BlockSpec((1, tm, n), lambda i, j: (i, j, 0))
    vec = pl.BlockSpec((1, 1, d), lambda i, j: (i, 0, 0))
    return pl.pallas_call(
        _gdn_in_kernel,
        grid=(b, s // tm),
        in_specs=[tile(d), _const_spec((1, d)), vec, vec, _pick_spec((l,), (d, 4 * width)), _const_spec(w_ba.shape)],
        out_specs=[tile(3 * width), tile(width), tile(2 * LANE)],
        out_shape=[jax.ShapeDtypeStruct((b, s, 3 * width), F32),
                   jax.ShapeDtypeStruct((b, s, width), F32),
                   jax.ShapeDtypeStruct((b, s, 2 * LANE), F32)],
        compiler_params=_cparams("parallel", "parallel"),
        name="gdn_in",
    )(x, norm_g.reshape(1, d), shift, scale, w_all, w_ba)


def _split2(t):
    hi = t.astype(BF16)
    return hi, (t - hi.astype(F32)).astype(BF16)


def _dot2(pieces, m):
    return _dot(pieces[0], m) + _dot(pieces[1], m)


def _dot2_left(m, pieces):
    return _dot(m, pieces[0]) + _dot(m, pieces[1])


def _tile_rows(t, n):
    return jnp.concatenate([t] * n, axis=0)


def _iota2(shape):
    return lax.broadcasted_iota(jnp.int32, shape, 0), lax.broadcasted_iota(jnp.int32, shape, 1)


GDN_GROUP = 4


def _gdn_intra_kernel(qkv_ref, ba_ref, convw_ref, alog_ref, dtb_ref,
                      u_ref, w_ref, qe_ref, kd_ref, attn_ref, egl_ref, tail_ref, *, heads, nchunk):
    c = CHUNK
    dk = GDN_HEAD_DIM
    width = heads * dk
    ts = nchunk * c
    gw = GDN_GROUP * dk
    pw_ = GDN_GROUP * c
    ngroup = heads // GDN_GROUP

    @pl.when(pl.program_id(1) == 0)
    def _():
        tail_ref[...] = jnp.zeros(tail_ref.shape, F32)

    cur = qkv_ref[0]
    prev = tail_ref[...]
    k_taps = convw_ref.shape[0]
    acc = cur * convw_ref[k_taps - 1:k_taps, :]
    cur3 = cur.reshape(ts // 8, 8, 3 * width)
    row3 = lax.broadcasted_iota(jnp.int32, (1, 8, 1), 1)
    for j in range(k_taps - 1):
        d = k_taps - 1 - j
        rot = pltpu.roll(cur3, d, 1)
        above = jnp.concatenate([pltpu.roll(prev, d, 0)[None], rot[:-1]], axis=0)
        acc = acc + jnp.where(row3 < d, above, rot).reshape(ts, 3 * width) * convw_ref[j:j + 1, :]
    tail_ref[...] = cur[ts - 8:]
    act = _silu(acc)

    def l2n(t, scale):
        parts = []
        for h in range(heads):
            th = t[:, h * dk:(h + 1) * dk]
            parts.append(th * (lax.rsqrt(jnp.sum(th * th, axis=-1, keepdims=True) + EPS) * scale))
        return jnp.concatenate(parts, axis=1)

    qn = l2n(act[:, :width], dk ** -0.5)
    kn = l2n(act[:, width:2 * width], 1.0)
    v16 = act[:, 2 * width:].astype(BF16)
    k16 = kn.astype(BF16)
    q16 = qn.astype(BF16)

    ba = ba_ref[0]
    beta = 1.0 / (1.0 + jnp.exp(-ba[:, :LANE]))
    al = ba[:, LANE:] + dtb_ref[...]
    softplus = jnp.maximum(al, 0.0) + jnp.log1p(jnp.exp(-jnp.abs(al)))
    lane = lax.broadcasted_iota(jnp.int32, (ts, LANE), 1)
    gl = jnp.where(lane < heads, -jnp.exp(alog_ref[...]) * softplus, 0.0)

    r, cc = _iota2((ts, ts))
    tri = ((r // c == cc // c) & (r >= cc)).astype(BF16)
    gcum = _dot2_left(tri, _split2(gl))
    r, cc = _iota2((LANE, width))
    e_full = (r == cc // dk).astype(BF16)
    r, cc = _iota2((LANE, heads * c))
    e_pack = (r == cc // c).astype(BF16)
    gc_pieces = _split2(gcum)
    gc_full = _dot2(gc_pieces, e_full)
    gc_pack = _dot2(gc_pieces, e_pack)
    beta_pack = _dot(beta.astype(BF16), e_pack)

    qe_ref[0] = (qn * jnp.exp(gc_full)).astype(qe_ref.dtype)

    r, cc = _iota2((c, pw_))
    eye_p = r == cc % c
    incl_p = r >= cc % c
    strict_p = r > cc % c
    r, cc = _iota2((pw_, gw))
    bd_wide = r // c == cc // dk
    r, cc = _iota2((pw_, pw_))
    bd_sq = r // c == cc // c
    zero16 = jnp.zeros((), BF16)

    units = [(ci, g) for ci in range(nchunk) for g in range(ngroup)]
    rows = lambda ci: slice(ci * c, (ci + 1) * c)
    gsl = lambda g: slice(g * gw, (g + 1) * gw)
    psl = lambda g: slice(g * pw_, (g + 1) * pw_)

    for ci in range(nchunk):
        g_last = gc_full[ci * c + c - 1:ci * c + c, :]
        egl_ref[0, ci] = jnp.exp(g_last)
        kd_ref[0, rows(ci), :] = (kn[rows(ci)] * jnp.exp(g_last - gc_full[rows(ci)])).astype(kd_ref.dtype)

    bd_k = [jnp.where(bd_wide, _tile_rows(k16[rows(ci), gsl(g)], GDN_GROUP), zero16) for ci, g in units]
    sc = [_dot_nt(jnp.concatenate([k16[rows(ci), gsl(g)], q16[rows(ci), gsl(g)]], axis=0), bd_k[i])
          for i, (ci, g) in enumerate(units)]
    a_p, beta_r, eg_r = [], [], []
    for i, (ci, g) in enumerate(units):
        gcp = gc_pack[rows(ci), psl(g)]
        gr = jnp.sum(jnp.where(eye_p, gcp, 0.0), axis=0, keepdims=True)
        bp = beta_pack[rows(ci), psl(g)]
        beta_r.append(jnp.sum(jnp.where(eye_p, bp, 0.0), axis=0, keepdims=True))
        eg_r.append(jnp.exp(gr))
        decay = jnp.where(incl_p, jnp.exp(jnp.where(incl_p, gcp - gr, 0.0)), 0.0)
        a_p.append(jnp.where(strict_p, sc[i][:c] * bp * decay, 0.0))
        attn_ref[0, rows(ci), psl(g)] = jnp.where(incl_p, sc[i][c:] * decay, 0.0).astype(attn_ref.dtype)

    def bd(t16):
        return jnp.where(bd_sq, _tile_rows(t16, GDN_GROUP), zero16)

    t_inv = [jnp.where(eye_p, 1.0, 0.0) - a for a in a_p]
    a16 = [a.astype(BF16) for a in a_p]
    pw = [_dot(a16[i], bd(a16[i])) for i in range(len(units))]
    n_sq = c.bit_length() - 2
    for step in range(n_sq):
        pw16 = [p.astype(BF16) for p in pw]
        if step < n_sq - 1:
            prod = [_dot(jnp.concatenate([t_inv[i].astype(BF16), pw16[i]], axis=0), bd(pw16[i]))
                    for i in range(len(units))]
            t_inv = [t_inv[i] + prod[i][:c] for i in range(len(units))]
            pw = [prod[i][c:] for i in range(len(units))]
        else:
            t_inv = [t_inv[i] + _dot(t_inv[i].astype(BF16), bd(pw16[i])) for i in range(len(units))]

    for i, (ci, g) in enumerate(units):
        t_u = (t_inv[i] * beta_r[i]).astype(BF16)
        t_w = (t_inv[i] * (beta_r[i] * eg_r[i])).astype(BF16)
        bd_v = jnp.where(bd_wide, _tile_rows(v16[rows(ci), gsl(g)], GDN_GROUP), zero16)
        u_ref[0, rows(ci), gsl(g)] = _dot(t_u, bd_v).astype(u_ref.dtype)
        w_ref[0, rows(ci), gsl(g)] = _dot(t_w, bd_k[i]).astype(w_ref.dtype)


def _gdn_seq_kernel(u_ref, w_ref, qe_ref, kd_ref, attn_ref, egl_ref, z_ref, ng_ref, o_ref, state_ref,
                    *, heads, nchunk):
    c = CHUNK
    dk = GDN_HEAD_DIM
    gw = GDN_GROUP * dk
    pw_ = GDN_GROUP * c
    ngroup = heads // GDN_GROUP

    @pl.when(pl.program_id(1) == 0)
    def _():
        state_ref[...] = jnp.zeros_like(state_ref)

    r, cc = _iota2((pw_, gw))
    bd_wide = r // c == cc // dk
    zero16 = jnp.zeros((), BF16)
    hsl = lambda h: slice(h * dk, (h + 1) * dk)
    for ci in range(nchunk):
        rows = slice(ci * c, (ci + 1) * c)
        st16 = [state_ref[h].astype(BF16) for h in range(heads)]
        zero = jnp.zeros((dk, dk), BF16)
        rd = []
        for h in range(0, heads, 2):
            psl = slice(h * dk, (h + 2) * dk)
            st_pair = jnp.concatenate([jnp.concatenate([st16[h], zero], axis=1),
                                       jnp.concatenate([zero, st16[h + 1]], axis=1)], axis=0)
            pair = _dot(jnp.concatenate([w_ref[0, rows, psl], qe_ref[0, rows, psl]], axis=0), st_pair)
            rd += [pair[:, :dk], pair[:, dk:]]
        vn16 = [(u_ref[0, rows, hsl(h)] - rd[h][:c]).astype(BF16) for h in range(heads)]
        intra = []
        for g in range(ngroup):
            vg = jnp.concatenate(vn16[g * GDN_GROUP:(g + 1) * GDN_GROUP], axis=1)
            bd_vn = jnp.where(bd_wide, _tile_rows(vg, GDN_GROUP), zero16)
            intra.append(_dot(attn_ref[0, rows, g * pw_:(g + 1) * pw_], bd_vn))
        for h in range(heads):
            state_ref[h] = state_ref[h] * egl_ref[0, ci, :, hsl(h)] + _dot_tn(kd_ref[0, rows, hsl(h)], vn16[h])
        for h in range(heads):
            g, hh = divmod(h, GDN_GROUP)
            o = rd[h][c:] + intra[g][:, hh * dk:(hh + 1) * dk]
            o = o * lax.rsqrt(jnp.mean(o * o, axis=-1, keepdims=True) + EPS) * ng_ref[...]
            o_ref[0, rows, hsl(h)] = (o * _silu(z_ref[0, rows, hsl(h)])).astype(o_ref.dtype)


def _gdn_core(qkv, z, ba, conv_w, a_log, dt_bias, norm_g, nchunk=4):
    b, s, width = z.shape
    heads = width // GDN_HEAD_DIM
    ts = nchunk * CHUNK
    pad = lambda t: jnp.pad(t.reshape(1, heads), ((0, 0), (0, LANE - heads)))
    tile = lambda n: pl.BlockSpec((1, ts, n), lambda i, j: (i, j, 0))
    egl_spec = pl.BlockSpec((1, nchunk, 1, width), lambda i, j: (i, j, 0, 0))
    packed = heads * CHUNK
    u, w, qe, kd, attn, egl = pl.pallas_call(
        functools.partial(_gdn_intra_kernel, heads=heads, nchunk=nchunk),
        grid=(b, s // ts),
        in_specs=[tile(3 * width), tile(2 * LANE), _const_spec(conv_w.shape),
                  _const_spec((1, LANE)), _const_spec((1, LANE))],
        out_specs=[tile(width), tile(width), tile(width), tile(width), tile(packed), egl_spec],
        out_shape=[jax.ShapeDtypeStruct((b, s, width), F32),
                   jax.ShapeDtypeStruct((b, s, width), BF16),
                   jax.ShapeDtypeStruct((b, s, width), BF16),
                   jax.ShapeDtypeStruct((b, s, width), BF16),
                   jax.ShapeDtypeStruct((b, s, packed), BF16),
                   jax.ShapeDtypeStruct((b, s // CHUNK, 1, width), F32)],
        scratch_shapes=[pltpu.VMEM((8, 3 * width), F32)],
        compiler_params=_cparams("parallel", "arbitrary"),
        name="gdn_intra",
    )(qkv, ba, conv_w, pad(a_log), pad(dt_bias))
    return pl.pallas_call(
        functools.partial(_gdn_seq_kernel, heads=heads, nchunk=nchunk),
        grid=(b, s // ts),
        in_specs=[tile(width), tile(width), tile(width), tile(width), tile(packed), egl_spec,
                  tile(width), _const_spec((1, GDN_HEAD_DIM))],
        out_specs=tile(width),
        out_shape=jax.ShapeDtypeStruct((b, s, width), BF16),
        scratch_shapes=[pltpu.VMEM((heads, GDN_HEAD_DIM, GDN_HEAD_DIM), F32)],
        compiler_params=_cparams("parallel", "arbitrary"),
        name="gdn_seq",
    )(u, w, qe, kd, attn, egl, z, norm_g.reshape(1, GDN_HEAD_DIM))


def _kv_kernel(x_ref, g_ref, sh_ref, sc_ref, wdl_ref, wdr_ref, lg_ref, wk_ref, wv_ref,
               kgn_ref, kgr_ref, cos_ref, sin_ref, k_ref, vt_ref, *, heads):
    hb = _modulated(x_ref[0], g_ref[...], sh_ref[0], sc_ref[0]).astype(BF16)
    lat = _dot(hb, wdl_ref[...])
    rope = _dot(hb, wdr_ref[...])
    lat = lat * lax.rsqrt(jnp.mean(lat * lat, axis=-1, keepdims=True) + EPS) * lg_ref[...]
    lat16 = lat.astype(BF16)
    k_nope = _dot(lat16, wk_ref[...])
    v = _dot(lat16, wv_ref[...])
    rope_sq = jnp.sum(rope * rope, axis=-1, keepdims=True)
    for h in range(heads):
        sl = slice(h * QK_NOPE, (h + 1) * QK_NOPE)
        kn = k_nope[:, sl]
        inv = lax.rsqrt((jnp.sum(kn * kn, axis=-1, keepdims=True) + rope_sq) / (QK_NOPE + QK_ROPE) + EPS)
        k_ref[0, h, :, :QK_NOPE] = (kn * inv * kgn_ref[...]).astype(k_ref.dtype)
        kr = _rotate(rope * inv * kgr_ref[...], cos_ref[0], sin_ref[0])
        k_ref[0, h, :, QK_NOPE:] = kr.astype(k_ref.dtype)
        vt_ref[0, h, 0, :V_HEAD, :] = jnp.transpose(v[:, h * V_HEAD:(h + 1) * V_HEAD]).astype(vt_ref.dtype)
        vt_ref[0, h, 0, V_HEAD:, :] = jnp.ones((V_AUG - V_HEAD, vt_ref.shape[-1]), vt_ref.dtype)


def _shared_kv(x, norm_g, shift, scale, w_dl, w_dr, lat_g, w_k, w_v, kg_nope, kg_rope, cos, sin, tm=512):
    b, s, d = x.shape
    heads = w_k.shape[1] // QK_NOPE
    tile = lambda n: pl.BlockSpec((1, tm, n), lambda i, j: (i, j, 0))
    vec = pl.BlockSpec((1, 1, d), lambda i, j: (i, 0, 0))
    return pl.pallas_call(
        functools.partial(_kv_kernel, heads=heads),
        grid=(b, s // tm),
        in_specs=[tile(d), _const_spec((1, d)), vec, vec, _const_spec(w_dl.shape), _const_spec(w_dr.shape),
                  _const_spec(lat_g.shape), _const_spec(w_k.shape), _const_spec(w_v.shape),
                  _const_spec(kg_nope.shape), _const_spec(kg_rope.shape), tile(LANE), tile(LANE)],
        out_specs=[pl.BlockSpec((1, heads, tm, QK_PAD), lambda i, j: (i, 0, j, 0)),
                   pl.BlockSpec((1, heads, 1, V_AUG, tm), lambda i, j: (i, 0, j, 0, 0))],
        out_shape=[jax.ShapeDtypeStruct((b, heads, s, QK_PAD), BF16),
                   jax.ShapeDtypeStruct((b, heads, s // tm, V_AUG, tm), BF16)],
        compiler_params=_cparams("parallel", "parallel"),
        name="mla_kv",
    )(x, norm_g.reshape(1, d), shift, scale, w_dl, w_dr, lat_g, w_k, w_v, kg_nope, kg_rope, cos, sin)


def _q_kernel(x_ref, g_ref, sh_ref, sc_ref, wdq_ref, qlg_ref, wqn_ref, wqr_ref,
              qgn_ref, qgr_ref, cos_ref, sin_ref, q_ref, *, heads):
    hb = _modulated(x_ref[0], g_ref[...], sh_ref[0], sc_ref[0]).astype(BF16)
    ql = _dot(hb, wdq_ref[...])
    ql = ql * lax.rsqrt(jnp.mean(ql * ql, axis=-1, keepdims=True) + EPS) * qlg_ref[...]
    ql16 = ql.astype(BF16)
    q_nope = _dot(ql16, wqn_ref[...])
    q_rope = _dot(ql16, wqr_ref[...])
    sm_scale = (QK_NOPE + QK_ROPE) ** -0.5 * 1.4426950408889634
    for h in range(heads):
        sl = slice(h * LANE, (h + 1) * LANE)
        qn, qr = q_nope[:, sl], q_rope[:, sl]
        ssq = jnp.sum(qn * qn, axis=-1, keepdims=True) + jnp.sum(qr * qr, axis=-1, keepdims=True)
        inv = lax.rsqrt(ssq / (QK_NOPE + QK_ROPE) + EPS) * sm_scale
        q_ref[0, h, :, :QK_NOPE] = (qn * inv * qgn_ref[...]).astype(q_ref.dtype)
        q_ref[0, h, :, QK_NOPE:] = _rotate(qr * inv * qgr_ref[...], cos_ref[0], sin_ref[0]).astype(q_ref.dtype)


def _mla_q(x, norm_g, shift, scale, w_dq, ql_g, w_qn, w_qr, qg_nope, qg_rope, cos, sin, tm=512):
    b, s, d = x.shape
    heads = w_qn.shape[1] // QK_NOPE
    tile = lambda n: pl.BlockSpec((1, tm, n), lambda i, j: (i, j, 0))
    vec = pl.BlockSpec((1, 1, d), lambda i, j: (i, 0, 0))
    return pl.pallas_call(
        functools.partial(_q_kernel, heads=heads),
        grid=(b, s // tm),
        in_specs=[tile(d), _const_spec((1, d)), vec, vec, _const_spec(w_dq.shape), _const_spec(ql_g.shape),
                  _const_spec(w_qn.shape), _const_spec(w_qr.shape), _const_spec(qg_nope.shape),
                  _const_spec(qg_rope.shape), tile(LANE), tile(LANE)],
        out_specs=pl.BlockSpec((1, heads, tm, QK_PAD), lambda i, j: (i, 0, j, 0)),
        out_shape=jax.ShapeDtypeStruct((b, heads, s, QK_PAD), BF16),
        compiler_params=_cparams("parallel", "parallel"),
        name="mla_q",
    )(x, norm_g.reshape(1, d), shift, scale, w_dq, ql_g, w_qn, w_qr, qg_nope, qg_rope, cos, sin)


def _attn_kernel(q_ref, k_ref, vt_ref, o_ref, s_ref, p_ref, m_ref, al_ref, acc_ref, *, tq):
    nq = q_ref.shape[2] // tq
    pairs = [(qi, j) for qi in range(nq) for j in range(qi + 1)]

    def scores(slot, qi, j):
        s_ref[slot] = _dot_nt(k_ref[0, 0, j * tq:(j + 1) * tq, :], q_ref[0, 0, qi * tq:(qi + 1) * tq, :])

    def softmax(slot, qi, j):
        s = s_ref[slot]
        if j == qi:
            key_c = lax.broadcasted_iota(jnp.int32, (tq, tq), 0) // CHUNK
            qry_c = lax.broadcasted_iota(jnp.int32, (tq, tq), 1) // CHUNK
            s = jnp.where(key_c <= qry_c, s, -jnp.inf)
        m_new = jnp.max(s, axis=0, keepdims=True)
        if j > 0:
            m = m_ref[...]
            m_new = jnp.maximum(m, m_new)
            al_ref[slot] = jnp.exp2(m - m_new)
        p_ref[slot] = jnp.exp2(s - m_new).astype(BF16)
        if j < qi:
            m_ref[...] = m_new

    def value(slot, qi, j):
        acc = _dot(vt_ref[0, 0, j], p_ref[slot])
        if j > 0:
            acc = al_ref[slot] * acc_ref[...] + acc
        if j == qi:
            out = acc[:V_HEAD] / acc[V_HEAD:V_HEAD + 1]
            o_ref[0, qi * tq:(qi + 1) * tq, :] = jnp.transpose(out).astype(o_ref.dtype)
        else:
            acc_ref[...] = acc

    scores(0, *pairs[0])
    for t, pair in enumerate(pairs):
        if t + 1 < len(pairs):
            scores((t + 1) % 2, *pairs[t + 1])
        if t > 0:
            value((t - 1) % 2, *pairs[t - 1])
        softmax(t % 2, *pair)
    value((len(pairs) - 1) % 2, *pairs[-1])


def _attention(q, k, vt):
    b, heads, s, dq = q.shape
    tq = vt.shape[-1]
    nkb = s // tq
    whole = lambda i, h: (i, h, 0, 0)
    return pl.pallas_call(
        functools.partial(_attn_kernel, tq=tq),
        grid=(b, heads),
        in_specs=[pl.BlockSpec((1, 1, s, dq), whole),
                  pl.BlockSpec((1, 1, s, dq), whole),
                  pl.BlockSpec((1, 1, nkb, V_AUG, tq), lambda i, h: (i, h, 0, 0, 0))],
        out_specs=pl.BlockSpec((1, s, V_HEAD), lambda i, h: (i, 0, h)),
        out_shape=jax.ShapeDtypeStruct((b, s, heads * V_HEAD), BF16),
        scratch_shapes=[pltpu.VMEM((2, tq, tq), F32), pltpu.VMEM((2, tq, tq), BF16), pltpu.VMEM((1, tq), F32),
                        pltpu.VMEM((2, 1, tq), F32), pltpu.VMEM((V_AUG, tq), F32)],
        compiler_params=_cparams("parallel", "parallel"),
        name="mla_attn",
    )(q, k, vt)


def _pad_cols(w, n):
    return jnp.pad(w, ((0, 0), (0, n - w.shape[1])))


def kernel(x, c, positions, ada_w, ada_b, norm_g, ffn_w_in, ffn_w_out, gdn_w_in, gdn_conv_w, gdn_a_log,
           gdn_dt_bias, gdn_norm_g, gdn_w_out, kv_ada_w, kv_ada_b, kv_norm_g, mla_w_dkv, mla_kv_norm_g,
           mla_w_ukv, mla_k_norm_g, mla_w_dq, mla_q_lora_norm_g, mla_w_uq, mla_q_norm_g, mla_w_out):
    b, s, d = x.shape
    depth = ada_w.shape[0]
    n_a = gdn_w_in.shape[0]
    n_mod = ada_w.shape[2] // d
    width = gdn_w_out.shape[1]
    g_heads = width // GDN_HEAD_DIM
    kv_lora = mla_kv_norm_g.shape[0]
    m_heads = mla_w_ukv.shape[1] // (QK_NOPE + V_HEAD)

    c_pad = jnp.pad(c, ((0, 8 - b), (0, 0)))
    mod = _modulation(c_pad, ada_w, ada_b)[:, :b].reshape(depth, b, n_mod, 1, d)
    kv_mod = _modulation(c_pad, kv_ada_w[None], kv_ada_b[None])[0, :b].reshape(b, 2, 1, d)
    cos, sin = _rope_tables(positions)

    w_in = ffn_w_in.astype(BF16)
    w_out = ffn_w_out.astype(BF16)
    gdn_w = gdn_w_in.astype(BF16)
    gdn_wo = gdn_w_out.astype(BF16)
    mla_wo = mla_w_out.astype(BF16)
    k_sh = vt_sh = None
    for l in range(depth):
        m = lambda i: mod[l, :, i]
        x = _ffn(x, norm_g[l, 0], m(0), m(1), m(2), w_in, w_out, (l, 0))
        if l < n_a:
            w = gdn_w_in[l]
            w_ba = jnp.concatenate([_pad_cols(w[:, 4 * width:4 * width + g_heads], LANE),
                                    _pad_cols(w[:, 4 * width + g_heads:], LANE)], axis=1).astype(BF16)
            qkv, z, ba = _gdn_in(x, norm_g[l, 1], m(3), m(4), gdn_w, l, width, w_ba)
            y = _gdn_core(qkv, z, ba, gdn_conv_w[l], gdn_a_log[l], gdn_dt_bias[l], gdn_norm_g[l])
            pre = (y, gdn_wo, l, m(5))
        else:
            j = l - n_a
            w_uq = mla_w_uq[j].reshape(-1, m_heads, QK_NOPE + QK_ROPE)
            w_qn = w_uq[:, :, :QK_NOPE].reshape(-1, m_heads * QK_NOPE).astype(BF16)
            w_qr = jnp.pad(w_uq[:, :, QK_NOPE:], ((0, 0), (0, 0), (0, LANE - QK_ROPE)))
            w_qr = w_qr.reshape(-1, m_heads * LANE).astype(BF16)
            qg = mla_q_norm_g[j]
            q = _mla_q(x, norm_g[l, 1], m(3), m(4), mla_w_dq[j].astype(BF16),
                       mla_q_lora_norm_g[j].reshape(1, -1), w_qn, w_qr,
                       qg[:QK_NOPE].reshape(1, -1), _pad_cols(qg[QK_NOPE:].reshape(1, -1), LANE), cos, sin)
            y = _attention(q, k_sh, vt_sh)
            pre = (y, mla_wo, j, m(5))
        x = _ffn(x, norm_g[l, 2], m(6), m(7), m(8), w_in, w_out, (l, 1), pre=pre)
        if l == n_a - 1:
            w_ukv = mla_w_ukv.reshape(kv_lora, m_heads, QK_NOPE + V_HEAD)
            w_k = w_ukv[:, :, :QK_NOPE].reshape(kv_lora, m_heads * QK_NOPE).astype(BF16)
            w_v = w_ukv[:, :, QK_NOPE:].reshape(kv_lora, m_heads * V_HEAD).astype(BF16)
            kg = mla_k_norm_g
            k_sh, vt_sh = _shared_kv(
                x, kv_norm_g, kv_mod[:, 0], kv_mod[:, 1], mla_w_dkv[:, :kv_lora].astype(BF16),
                _pad_cols(mla_w_dkv[:, kv_lora:], LANE).astype(BF16), mla_kv_norm_g.reshape(1, -1), w_k, w_v,
                kg[:QK_NOPE].reshape(1, -1), _pad_cols(kg[QK_NOPE:].reshape(1, -1), LANE), cos, sin)
    return x
```

```python
import functools

import jax
import jax.numpy as jnp
from jax import lax
from jax.experimental import pallas as pl
from jax.experimental.pallas import tpu as pltpu

F32 = jnp.float32
BF16 = jnp.bfloat16

EPS = 1e-6
CHUNK = 64
ROPE_BASE = 10000.0
LANE = 128
VMEM_LIMIT = 56 * 1024 * 1024

GDN_HEAD_DIM = 128
QK_NOPE = 128
QK_ROPE = 64
V_HEAD = 128
V_AUG = V_HEAD + 16
QK_PAD = 256


def _cparams(*sem):
    return pltpu.CompilerParams(dimension_semantics=sem, vmem_limit_bytes=VMEM_LIMIT)


def _silu(t):
    half = 0.5 * t
    return half + half * jnp.tanh(half)


def _dot(a, b):
    return jnp.dot(a, b, preferred_element_type=F32)


def _dot_nt(a, b):
    return lax.dot_general(a, b, (((1,), (1,)), ((), ())), preferred_element_type=F32)


def _dot_tn(a, b):
    return lax.dot_general(a, b, (((0,), (0,)), ((), ())), preferred_element_type=F32)


def _split3(t):
    hi = t.astype(BF16)
    r1 = t - hi.astype(F32)
    mid = r1.astype(BF16)
    lo = (r1 - mid.astype(F32)).astype(BF16)
    return hi, mid, lo


def _const_spec(shape):
    nd = len(shape)
    return pl.BlockSpec(shape, lambda *_: (0,) * nd, pipeline_mode=pl.Buffered(1))


def _pick_spec(lead, block, tail=None):
    index = tuple(lead) + tuple(tail or (0,) * len(block))
    return pl.BlockSpec((None,) * len(lead) + tuple(block), lambda *_: index, pipeline_mode=pl.Buffered(1))


def _modulated(x, g, shift, scale):
    ms = jnp.mean(x * x, axis=-1, keepdims=True)
    return (x * lax.rsqrt(ms + EPS)) * (g * (1.0 + scale)) + shift


def _mod_kernel(c_ref, w_ref, b_ref, o_ref):
    ca = _silu(c_ref[...])
    c_hi = ca.astype(BF16)
    c_lo = (ca - c_hi.astype(F32)).astype(BF16)
    w = w_ref[0]
    w_hi = w.astype(BF16)
    w_lo = (w - w_hi.astype(F32)).astype(BF16)
    o_ref[0] = _dot(c_hi, w_hi) + (_dot(c_lo, w_hi) + _dot(c_hi, w_lo)) + b_ref[0]


def _modulation(c_pad, w, b):
    nl, d, n = w.shape
    tn = 1024
    return pl.pallas_call(
        _mod_kernel,
        grid=(nl, n // tn),
        in_specs=[
            pl.BlockSpec((8, d), lambda l, j: (0, 0)),
            pl.BlockSpec((1, d, tn), lambda l, j: (l, 0, j)),
            pl.BlockSpec((1, 1, tn), lambda l, j: (l, 0, j)),
        ],
        out_specs=pl.BlockSpec((1, 8, tn), lambda l, j: (l, 0, j)),
        out_shape=jax.ShapeDtypeStruct((nl, 8, n), F32),
        compiler_params=_cparams("parallel", "parallel"),
        name="adaln_mod",
    )(c_pad, w, b.reshape(nl, 1, n))


def _rope_kernel(pos_ref, cos_ref, sin_ref):
    half = QK_ROPE // 2
    pos = pos_ref[0].astype(F32)
    lane = lax.broadcasted_iota(jnp.int32, (1, LANE), 1)
    idx = jnp.where(lane < half, lane, lane - half).astype(F32)
    inv_freq = jnp.exp(idx * (-jnp.log(ROPE_BASE) / half))
    ang = pos * inv_freq
    valid = lane < QK_ROPE
    cos_ref[0] = jnp.where(valid, jnp.cos(ang), 0.0)
    sin_ref[0] = jnp.where(valid, jnp.where(lane < half, -jnp.sin(ang), jnp.sin(ang)), 0.0)


def _rope_tables(positions):
    b, s = positions.shape
    ts = 512
    out = jax.ShapeDtypeStruct((b, s, LANE), F32)
    return pl.pallas_call(
        _rope_kernel,
        grid=(b, s // ts),
        in_specs=[pl.BlockSpec((1, ts, 1), lambda i, j: (i, j, 0))],
        out_specs=[pl.BlockSpec((1, ts, LANE), lambda i, j: (i, j, 0))] * 2,
        out_shape=[out, out],
        compiler_params=_cparams("parallel", "parallel"),
        name="rope_tables",
    )(positions.reshape(b, s, 1))


def _rotate(t, cos, sin):
    half = QK_ROPE // 2
    lane = lax.broadcasted_iota(jnp.int32, t.shape, 1)
    swapped = jnp.where(lane < half, pltpu.roll(t, LANE - half, 1), pltpu.roll(t, half, 1))
    return t * cos + swapped * sin


def _ffn_kernel(*refs, pre):
    if pre:
        (x_ref, y_ref, wo_ref, gmix_ref, g_ref, sh_ref, sc_ref, gate_ref,
         wg_ref, wu_ref, wd_ref, out_ref) = refs
    else:
        x_ref, g_ref, sh_ref, sc_ref, gate_ref, wg_ref, wu_ref, wd_ref, out_ref = refs
    x = x_ref[0]
    if pre:
        x = x + gmix_ref[0] * _dot(y_ref[0], wo_ref[...])
    hb = _modulated(x, g_ref[...], sh_ref[0], sc_ref[0]).astype(BF16)
    gate = _dot(hb, wg_ref[...])
    up = _dot(hb, wu_ref[...])
    act = _silu(gate.astype(BF16)) * up.astype(BF16)
    out_ref[0] = x + (0.5 * gate_ref[0]) * _dot(act, wd_ref[...])


def _ffn(x, norm_g, shift, scale, gate, w_in, w_out, idx, pre=None, tm=512):
    b, s, d = x.shape
    f = w_out.shape[-2]
    tile = pl.BlockSpec((1, tm, d), lambda i, j: (i, j, 0))
    vec = pl.BlockSpec((1, 1, d), lambda i, j: (i, 0, 0))
    args, specs = [x], [tile]
    if pre is not None:
        y, w_o, jo, g_mix = pre
        args += [y, w_o, g_mix]
        specs += [pl.BlockSpec((1, tm, y.shape[-1]), lambda i, j: (i, j, 0)), _pick_spec((jo,), w_o.shape[1:]), vec]
    args += [norm_g.reshape(1, d), shift, scale, gate, w_in, w_in, w_out]
    specs += [_const_spec((1, d)), vec, vec, vec,
              _pick_spec(idx, (d, f), (0, 0)), _pick_spec(idx, (d, f), (0, 1)), _pick_spec(idx, (f, d))]
    return pl.pallas_call(
        functools.partial(_ffn_kernel, pre=pre is not None),
        grid=(b, s // tm),
        in_specs=specs,
        out_specs=tile,
        out_shape=jax.ShapeDtypeStruct((b, s, d), F32),
        compiler_params=_cparams("parallel", "parallel"),
        name="ffn_pre" if pre is not None else "ffn",
    )(*args)


def _gdn_in_kernel(x_ref, g_ref, sh_ref, sc_ref, w_ref, wba_ref, qkv_ref, z_ref, ba_ref):
    hb = _modulated(x_ref[0], g_ref[...], sh_ref[0], sc_ref[0]).astype(BF16)
    width = z_ref.shape[-1]
    p = _dot(hb, w_ref[...])
    qkv_ref[0] = p[:, :3 * width]
    z_ref[0] = p[:, 3 * width:]
    ba_ref[0] = _dot(hb, wba_ref[...])


def _gdn_in(x, norm_g, shift, scale, w_all, l, width, w_ba, tm=512):
    b, s, d = x.shape
    tile = lambda n: pl.BlockSpec((1, tm, n), lambda i, j: (i, j, 0))
    vec = pl.BlockSpec((1, 1, d), lambda i, j: (i, 0, 0))
    return pl.pallas_call(
        _gdn_in_kernel,
        grid=(b, s // tm),
        in_specs=[tile(d), _const_spec((1, d)), vec, vec, _pick_spec((l,), (d, 4 * width)), _const_spec(w_ba.shape)],
        out_specs=[tile(3 * width), tile(width), tile(2 * LANE)],
        out_shape=[jax.ShapeDtypeStruct((b, s, 3 * width), F32),
                   jax.ShapeDtypeStruct((b, s, width), F32),
                   jax.ShapeDtypeStruct((b, s, 2 * LANE), F32)],
        compiler_params=_cparams("parallel", "parallel"),
        name="gdn_in",
    )(x, norm_g.reshape(1, d), shift, scale, w_all, w_ba)


def _split2(t):
    hi = t.astype(BF16)
    return hi, (t - hi.astype(F32)).astype(BF16)


def _dot2(pieces, m):
    return _dot(pieces[0], m) + _dot(pieces[1], m)


def _dot2_left(m, pieces):
    return _dot(m, pieces[0]) + _dot(m, pieces[1])


def _tile_rows(t, n):
    return jnp.concatenate([t] * n, axis=0)


def _iota2(shape):
    return lax.broadcasted_iota(jnp.int32, shape, 0), lax.broadcasted_iota(jnp.int32, shape, 1)


GDN_GROUP = 4


def _gdn_intra_kernel(qkv_ref, ba_ref, convw_ref, alog_ref, dtb_ref,
                      u_ref, w_ref, qe_ref, kd_ref, attn_ref, egl_ref, tail_ref, *, heads, nchunk):
    c = CHUNK
    dk = GDN_HEAD_DIM
    width = heads * dk
    ts = nchunk * c
    gw = GDN_GROUP * dk
    pw_ = GDN_GROUP * c
    ngroup = heads // GDN_GROUP

    @pl.when(pl.program_id(1) == 0)
    def _():
        tail_ref[...] = jnp.zeros(tail_ref.shape, F32)

    cur = qkv_ref[0]
    prev = tail_ref[...]
    k_taps = convw_ref.shape[0]
    acc = cur * convw_ref[k_taps - 1:k_taps, :]
    cur3 = cur.reshape(ts // 8, 8, 3 * width)
    row3 = lax.broadcasted_iota(jnp.int32, (1, 8, 1), 1)
    for j in range(k_taps - 1):
        d = k_taps - 1 - j
        rot = pltpu.roll(cur3, d, 1)
        above = jnp.concatenate([pltpu.roll(prev, d, 0)[None], rot[:-1]], axis=0)
        acc = acc + jnp.where(row3 < d, above, rot).reshape(ts, 3 * width) * convw_ref[j:j + 1, :]
    tail_ref[...] = cur[ts - 8:]
    act = _silu(acc)

    def l2n(t, scale):
        parts = []
        for h in range(heads):
            th = t[:, h * dk:(h + 1) * dk]
            parts.append(th * (lax.rsqrt(jnp.sum(th * th, axis=-1, keepdims=True) + EPS) * scale))
        return jnp.concatenate(parts, axis=1)

    qn = l2n(act[:, :width], dk ** -0.5)
    kn = l2n(act[:, width:2 * width], 1.0)
    v16 = act[:, 2 * width:].astype(BF16)
    k16 = kn.astype(BF16)
    q16 = qn.astype(BF16)

    ba = ba_ref[0]
    beta = 1.0 / (1.0 + jnp.exp(-ba[:, :LANE]))
    al = ba[:, LANE:] + dtb_ref[...]
    softplus = jnp.maximum(al, 0.0) + jnp.log1p(jnp.exp(-jnp.abs(al)))
    lane = lax.broadcasted_iota(jnp.int32, (ts, LANE), 1)
    gl = jnp.where(lane < heads, -jnp.exp(alog_ref[...]) * softplus, 0.0)

    r, cc = _iota2((ts, ts))
    tri = ((r // c == cc // c) & (r >= cc)).astype(BF16)
    gcum = _dot2_left(tri, _split2(gl))
    r, cc = _iota2((LANE, width))
    e_full = (r == cc // dk).astype(BF16)
    r, cc = _iota2((LANE, heads * c))
    e_pack = (r == cc // c).astype(BF16)
    gc_pieces = _split2(gcum)
    gc_full = _dot2(gc_pieces, e_full)
    gc_pack = _dot2(gc_pieces, e_pack)
    beta_pack = _dot(beta.astype(BF16), e_pack)

    qe_ref[0] = (qn * jnp.exp(gc_full)).astype(qe_ref.dtype)

    r, cc = _iota2((c, pw_))
    eye_p = r == cc % c
    incl_p = r >= cc % c
    strict_p = r > cc % c
    r, cc = _iota2((pw_, gw))
    bd_wide = r // c == cc // dk
    r, cc = _iota2((pw_, pw_))
    bd_sq = r // c == cc // c
    zero16 = jnp.zeros((), BF16)

    units = [(ci, g) for ci in range(nchunk) for g in range(ngroup)]
    rows = lambda ci: slice(ci * c, (ci + 1) * c)
    gsl = lambda g: slice(g * gw, (g + 1) * gw)
    psl = lambda g: slice(g * pw_, (g + 1) * pw_)

    for ci in range(nchunk):
        g_last = gc_full[ci * c + c - 1:ci * c + c, :]
        egl_ref[0, ci] = jnp.exp(g_last)
        kd_ref[0, rows(ci), :] = (kn[rows(ci)] * jnp.exp(g_last - gc_full[rows(ci)])).astype(kd_ref.dtype)

    bd_k = [jnp.where(bd_wide, _tile_rows(k16[rows(ci), gsl(g)], GDN_GROUP), zero16) for ci, g in units]
    sc = [_dot_nt(jnp.concatenate([k16[rows(ci), gsl(g)], q16[rows(ci), gsl(g)]], axis=0), bd_k[i])
          for i, (ci, g) in enumerate(units)]
    a_p, beta_r, eg_r = [], [], []
    for i, (ci, g) in enumerate(units):
        gcp = gc_pack[rows(ci), psl(g)]
        gr = jnp.sum(jnp.where(eye_p, gcp, 0.0), axis=0, keepdims=True)
        bp = beta_pack[rows(ci), psl(g)]
        beta_r.append(jnp.sum(jnp.where(eye_p, bp, 0.0), axis=0, keepdims=True))
        eg_r.append(jnp.exp(gr))
        decay = jnp.where(incl_p, jnp.exp(jnp.where(incl_p, gcp - gr, 0.0)), 0.0)
        a_p.append(jnp.where(strict_p, sc[i][:c] * bp * decay, 0.0))
        attn_ref[0, rows(ci), psl(g)] = jnp.where(incl_p, sc[i][c:] * decay, 0.0).astype(attn_ref.dtype)

    def bd(t16):
        return jnp.where(bd_sq, _tile_rows(t16, GDN_GROUP), zero16)

    t_inv = [jnp.where(eye_p, 1.0, 0.0) - a for a in a_p]
    a16 = [a.astype(BF16) for a in a_p]
    pw = [_dot(a16[i], bd(a16[i])) for i in range(len(units))]
    n_sq = c.bit_length() - 2
    for step in range(n_sq):
        pw16 = [p.astype(BF16) for p in pw]
        if step < n_sq - 1:
            prod = [_dot(jnp.concatenate([t_inv[i].astype(BF16), pw16[i]], axis=0), bd(pw16[i]))
                    for i in range(len(units))]
            t_inv = [t_inv[i] + prod[i][:c] for i in range(len(units))]
            pw = [prod[i][c:] for i in range(len(units))]
        else:
            t_inv = [t_inv[i] + _dot(t_inv[i].astype(BF16), bd(pw16[i])) for i in range(len(units))]

    for i, (ci, g) in enumerate(units):
        t_u = (t_inv[i] * beta_r[i]).astype(BF16)
        t_w = (t_inv[i] * (beta_r[i] * eg_r[i])).astype(BF16)
        bd_v = jnp.where(bd_wide, _tile_rows(v16[rows(ci), gsl(g)], GDN_GROUP), zero16)
        u_ref[0, rows(ci), gsl(g)] = _dot(t_u, bd_v).astype(u_ref.dtype)
        w_ref[0, rows(ci), gsl(g)] = _dot(t_w, bd_k[i]).astype(w_ref.dtype)


def _gdn_seq_kernel(u_ref, w_ref, qe_ref, kd_ref, attn_ref, egl_ref, z_ref, ng_ref, o_ref, state_ref,
                    *, heads, nchunk, nseq):
    c = CHUNK
    dk = GDN_HEAD_DIM
    gw = GDN_GROUP * dk
    pw_ = GDN_GROUP * c
    ngroup = heads // GDN_GROUP

    @pl.when(pl.program_id(1) == 0)
    def _():
        state_ref[...] = jnp.zeros_like(state_ref)

    r, cc = _iota2((pw_, gw))
    bd_wide = r // c == cc // dk
    zero16 = jnp.zeros((), BF16)
    zero = jnp.zeros((dk, dk), BF16)
    hsl = lambda h: slice(h * dk, (h + 1) * dk)
    units = [(bi, h) for bi in range(nseq) for h in range(heads)]
    for ci in range(nchunk):
        rows = slice(ci * c, (ci + 1) * c)
        st16 = {bh: state_ref[bh[0], bh[1]].astype(BF16) for bh in units}
        rd = {}
        for bi, h in units[::2]:
            psl = slice(h * dk, (h + 2) * dk)
            st_pair = jnp.concatenate([jnp.concatenate([st16[bi, h], zero], axis=1),
                                       jnp.concatenate([zero, st16[bi, h + 1]], axis=1)], axis=0)
            pair = _dot(jnp.concatenate([w_ref[bi, rows, psl], qe_ref[bi, rows, psl]], axis=0), st_pair)
            rd[bi, h], rd[bi, h + 1] = pair[:, :dk], pair[:, dk:]
        vn16 = {(bi, h): (u_ref[bi, rows, hsl(h)] - rd[bi, h][:c]).astype(BF16) for bi, h in units}
        intra = {}
        for bi in range(nseq):
            for g in range(ngroup):
                vg = jnp.concatenate([vn16[bi, h] for h in range(g * GDN_GROUP, (g + 1) * GDN_GROUP)], axis=1)
                bd_vn = jnp.where(bd_wide, _tile_rows(vg, GDN_GROUP), zero16)
                intra[bi, g] = _dot(attn_ref[bi, rows, g * pw_:(g + 1) * pw_], bd_vn)
        for bi, h in units:
            state_ref[bi, h] = (state_ref[bi, h] * egl_ref[bi, ci, :, hsl(h)]
                                + _dot_tn(kd_ref[bi, rows, hsl(h)], vn16[bi, h]))
        for bi, h in units:
            g, hh = divmod(h, GDN_GROUP)
            o = rd[bi, h][c:] + intra[bi, g][:, hh * dk:(hh + 1) * dk]
            o = o * lax.rsqrt(jnp.mean(o * o, axis=-1, keepdims=True) + EPS) * ng_ref[...]
            o_ref[bi, rows, hsl(h)] = (o * _silu(z_ref[bi, rows, hsl(h)])).astype(o_ref.dtype)


def _gdn_core(qkv, z, ba, conv_w, a_log, dt_bias, norm_g, nchunk=4):
    b, s, width = z.shape
    heads = width // GDN_HEAD_DIM
    ts = nchunk * CHUNK
    pad = lambda t: jnp.pad(t.reshape(1, heads), ((0, 0), (0, LANE - heads)))
    tile = lambda n: pl.BlockSpec((1, ts, n), lambda i, j: (i, j, 0))
    egl_spec = pl.BlockSpec((1, nchunk, 1, width), lambda i, j: (i, j, 0, 0))
    packed = heads * CHUNK
    u, w, qe, kd, attn, egl = pl.pallas_call(
        functools.partial(_gdn_intra_kernel, heads=heads, nchunk=nchunk),
        grid=(b, s // ts),
        in_specs=[tile(3 * width), tile(2 * LANE), _const_spec(conv_w.shape),
                  _const_spec((1, LANE)), _const_spec((1, LANE))],
        out_specs=[tile(width), tile(width), tile(width), tile(width), tile(packed), egl_spec],
        out_shape=[jax.ShapeDtypeStruct((b, s, width), F32),
                   jax.ShapeDtypeStruct((b, s, width), BF16),
                   jax.ShapeDtypeStruct((b, s, width), BF16),
                   jax.ShapeDtypeStruct((b, s, width), BF16),
                   jax.ShapeDtypeStruct((b, s, packed), BF16),
                   jax.ShapeDtypeStruct((b, s // CHUNK, 1, width), F32)],
        scratch_shapes=[pltpu.VMEM((8, 3 * width), F32)],
        compiler_params=_cparams("parallel", "arbitrary"),
        name="gdn_intra",
    )(qkv, ba, conv_w, pad(a_log), pad(dt_bias))
    nseq = 2 if b % 2 == 0 else 1
    seq_tile = lambda n: pl.BlockSpec((nseq, ts, n), lambda i, j: (i, j, 0))
    return pl.pallas_call(
        functools.partial(_gdn_seq_kernel, heads=heads, nchunk=nchunk, nseq=nseq),
        grid=(b // nseq, s // ts),
        in_specs=[seq_tile(width), seq_tile(width), seq_tile(width), seq_tile(width), seq_tile(packed),
                  pl.BlockSpec((nseq, nchunk, 1, width), lambda i, j: (i, j, 0, 0)),
                  seq_tile(width), _const_spec((1, GDN_HEAD_DIM))],
        out_specs=seq_tile(width),
        out_shape=jax.ShapeDtypeStruct((b, s, width), BF16),
        scratch_shapes=[pltpu.VMEM((nseq, heads, GDN_HEAD_DIM, GDN_HEAD_DIM), F32)],
        compiler_params=_cparams("parallel", "arbitrary"),
        name="gdn_seq",
    )(u, w, qe, kd, attn, egl, z, norm_g.reshape(1, GDN_HEAD_DIM))


def _kv_kernel(x_ref, g_ref, sh_ref, sc_ref, wdl_ref, wdr_ref, lg_ref, wk_ref, wv_ref,
               kgn_ref, kgr_ref, cos_ref, sin_ref, k_ref, vt_ref, *, heads):
    hb = _modulated(x_ref[0], g_ref[...], sh_ref[0], sc_ref[0]).astype(BF16)
    lat = _dot(hb, wdl_ref[...])
    rope = _dot(hb, wdr_ref[...])
    lat = lat * lax.rsqrt(jnp.mean(lat * lat, axis=-1, keepdims=True) + EPS) * lg_ref[...]
    lat16 = lat.astype(BF16)
    k_nope = _dot(lat16, wk_ref[...])
    v = _dot(lat16, wv_ref[...])
    rope_sq = jnp.sum(rope * rope, axis=-1, keepdims=True)
    for h in range(heads):
        sl = slice(h * QK_NOPE, (h + 1) * QK_NOPE)
        kn = k_nope[:, sl]
        inv = lax.rsqrt((jnp.sum(kn * kn, axis=-1, keepdims=True) + rope_sq) / (QK_NOPE + QK_ROPE) + EPS)
        k_ref[0, h, :, :QK_NOPE] = (kn * inv * kgn_ref[...]).astype(k_ref.dtype)
        kr = _rotate(rope * inv * kgr_ref[...], cos_ref[0], sin_ref[0])
        k_ref[0, h, :, QK_NOPE:] = kr.astype(k_ref.dtype)
        vt_ref[0, h, 0, :V_HEAD, :] = jnp.transpose(v[:, h * V_HEAD:(h + 1) * V_HEAD]).astype(vt_ref.dtype)
        vt_ref[0, h, 0, V_HEAD:, :] = jnp.ones((V_AUG - V_HEAD, vt_ref.shape[-1]), vt_ref.dtype)


def _shared_kv(x, norm_g, shift, scale, w_dl, w_dr, lat_g, w_k, w_v, kg_nope, kg_rope, cos, sin, tm=512):
    b, s, d = x.shape
    heads = w_k.shape[1] // QK_NOPE
    tile = lambda n: pl.BlockSpec((1, tm, n), lambda i, j: (i, j, 0))
    vec = pl.BlockSpec((1, 1, d), lambda i, j: (i, 0, 0))
    return pl.pallas_call(
        functools.partial(_kv_kernel, heads=heads),
        grid=(b, s // tm),
        in_specs=[tile(d), _const_spec((1, d)), vec, vec, _const_spec(w_dl.shape), _const_spec(w_dr.shape),
                  _const_spec(lat_g.shape), _const_spec(w_k.shape), _const_spec(w_v.shape),
                  _const_spec(kg_nope.shape), _const_spec(kg_rope.shape), tile(LANE), tile(LANE)],
        out_specs=[pl.BlockSpec((1, heads, tm, QK_PAD), lambda i, j: (i, 0, j, 0)),
                   pl.BlockSpec((1, heads, 1, V_AUG, tm), lambda i, j: (i, 0, j, 0, 0))],
        out_shape=[jax.ShapeDtypeStruct((b, heads, s, QK_PAD), BF16),
                   jax.ShapeDtypeStruct((b, heads, s // tm, V_AUG, tm), BF16)],
        compiler_params=_cparams("parallel", "parallel"),
        name="mla_kv",
    )(x, norm_g.reshape(1, d), shift, scale, w_dl, w_dr, lat_g, w_k, w_v, kg_nope, kg_rope, cos, sin)


def _q_kernel(x_ref, g_ref, sh_ref, sc_ref, wdq_ref, qlg_ref, wqn_ref, wqr_ref,
              qgn_ref, qgr_ref, cos_ref, sin_ref, q_ref, *, heads):
    hb = _modulated(x_ref[0], g_ref[...], sh_ref[0], sc_ref[0]).astype(BF16)
    ql = _dot(hb, wdq_ref[...])
    ql = ql * lax.rsqrt(jnp.mean(ql * ql, axis=-1, keepdims=True) + EPS) * qlg_ref[...]
    ql16 = ql.astype(BF16)
    q_nope = _dot(ql16, wqn_ref[...])
    q_rope = _dot(ql16, wqr_ref[...])
    sm_scale = (QK_NOPE + QK_ROPE) ** -0.5 * 1.4426950408889634
    for h in range(heads):
        sl = slice(h * LANE, (h + 1) * LANE)
        qn, qr = q_nope[:, sl], q_rope[:, sl]
        ssq = jnp.sum(qn * qn, axis=-1, keepdims=True) + jnp.sum(qr * qr, axis=-1, keepdims=True)
        inv = lax.rsqrt(ssq / (QK_NOPE + QK_ROPE) + EPS) * sm_scale
        q_ref[0, h, :, :QK_NOPE] = (qn * inv * qgn_ref[...]).astype(q_ref.dtype)
        q_ref[0, h, :, QK_NOPE:] = _rotate(qr * inv * qgr_ref[...], cos_ref[0], sin_ref[0]).astype(q_ref.dtype)


def _mla_q(x, norm_g, shift, scale, w_dq, ql_g, w_qn, w_qr, qg_nope, qg_rope, cos, sin, tm=512):
    b, s, d = x.shape
    heads = w_qn.shape[1] // QK_NOPE
    tile = lambda n: pl.BlockSpec((1, tm, n), lambda i, j: (i, j, 0))
    vec = pl.BlockSpec((1, 1, d), lambda i, j: (i, 0, 0))
    return pl.pallas_call(
        functools.partial(_q_kernel, heads=heads),
        grid=(b, s // tm),
        in_specs=[tile(d), _const_spec((1, d)), vec, vec, _const_spec(w_dq.shape), _const_spec(ql_g.shape),
                  _const_spec(w_qn.shape), _const_spec(w_qr.shape), _const_spec(qg_nope.shape),
                  _const_spec(qg_rope.shape), tile(LANE), tile(LANE)],
        out_specs=pl.BlockSpec((1, heads, tm, QK_PAD), lambda i, j: (i, 0, j, 0)),
        out_shape=jax.ShapeDtypeStruct((b, heads, s, QK_PAD), BF16),
        compiler_params=_cparams("parallel", "parallel"),
        name="mla_q",
    )(x, norm_g.reshape(1, d), shift, scale, w_dq, ql_g, w_qn, w_qr, qg_nope, qg_rope, cos, sin)


def _attn_kernel(q_ref, k_ref, vt_ref, o_ref, s_ref, p_ref, m_ref, al_ref, acc_ref, *, tq):
    nq = q_ref.shape[2] // tq
    pairs = [(qi, j) for qi in range(nq) for j in range(qi + 1)]

    def scores(slot, qi, j):
        s_ref[slot] = _dot_nt(k_ref[0, 0, j * tq:(j + 1) * tq, :], q_ref[0, 0, qi * tq:(qi + 1) * tq, :])

    def softmax(slot, qi, j):
        s = s_ref[slot]
        if j == qi:
            key_c = lax.broadcasted_iota(jnp.int32, (tq, tq), 0) // CHUNK
            qry_c = lax.broadcasted_iota(jnp.int32, (tq, tq), 1) // CHUNK
            s = jnp.where(key_c <= qry_c, s, -jnp.inf)
        m_new = jnp.max(s, axis=0, keepdims=True)
        if j > 0:
            m = m_ref[...]
            m_new = jnp.maximum(m, m_new)
            al_ref[slot] = jnp.exp2(m - m_new)
        p_ref[slot] = jnp.exp2(s - m_new).astype(BF16)
        if j < qi:
            m_ref[...] = m_new

    def value(slot, qi, j):
        acc = _dot(vt_ref[0, 0, j], p_ref[slot])
        if j > 0:
            acc = al_ref[slot] * acc_ref[...] + acc
        if j == qi:
            out = acc[:V_HEAD] / acc[V_HEAD:V_HEAD + 1]
            o_ref[0, qi * tq:(qi + 1) * tq, :] = jnp.transpose(out).astype(o_ref.dtype)
        else:
            acc_ref[...] = acc

    scores(0, *pairs[0])
    for t, pair in enumerate(pairs):
        if t + 1 < len(pairs):
            scores((t + 1) % 2, *pairs[t + 1])
        if t > 0:
            value((t - 1) % 2, *pairs[t - 1])
        softmax(t % 2, *pair)
    value((len(pairs) - 1) % 2, *pairs[-1])


def _attention(q, k, vt):
    b, heads, s, dq = q.shape
    tq = vt.shape[-1]
    nkb = s // tq
    whole = lambda i, h: (i, h, 0, 0)
    return pl.pallas_call(
        functools.partial(_attn_kernel, tq=tq),
        grid=(b, heads),
        in_specs=[pl.BlockSpec((1, 1, s, dq), whole),
                  pl.BlockSpec((1, 1, s, dq), whole),
                  pl.BlockSpec((1, 1, nkb, V_AUG, tq), lambda i, h: (i, h, 0, 0, 0))],
        out_specs=pl.BlockSpec((1, s, V_HEAD), lambda i, h: (i, 0, h)),
        out_shape=jax.ShapeDtypeStruct((b, s, heads * V_HEAD), BF16),
        scratch_shapes=[pltpu.VMEM((2, tq, tq), F32), pltpu.VMEM((2, tq, tq), BF16), pltpu.VMEM((1, tq), F32),
                        pltpu.VMEM((2, 1, tq), F32), pltpu.VMEM((V_AUG, tq), F32)],
        compiler_params=_cparams("parallel", "parallel"),
        name="mla_attn",
    )(q, k, vt)


def _pad_cols(w, n):
    return jnp.pad(w, ((0, 0), (0, n - w.shape[1])))


def kernel(x, c, positions, ada_w, ada_b, norm_g, ffn_w_in, ffn_w_out, gdn_w_in, gdn_conv_w, gdn_a_log,
           gdn_dt_bias, gdn_norm_g, gdn_w_out, kv_ada_w, kv_ada_b, kv_norm_g, mla_w_dkv, mla_kv_norm_g,
           mla_w_ukv, mla_k_norm_g, mla_w_dq, mla_q_lora_norm_g, mla_w_uq, mla_q_norm_g, mla_w_out):
    b, s, d = x.shape
    depth = ada_w.shape[0]
    n_a = gdn_w_in.shape[0]
    n_mod = ada_w.shape[2] // d
    width = gdn_w_out.shape[1]
    g_heads = width // GDN_HEAD_DIM
    kv_lora = mla_kv_norm_g.shape[0]
    m_heads = mla_w_ukv.shape[1] // (QK_NOPE + V_HEAD)

    c_pad = jnp.pad(c, ((0, 8 - b), (0, 0)))
    mod = _modulation(c_pad, ada_w, ada_b)[:, :b].reshape(depth, b, n_mod, 1, d)
    kv_mod = _modulation(c_pad, kv_ada_w[None], kv_ada_b[None])[0, :b].reshape(b, 2, 1, d)
    cos, sin = _rope_tables(positions)

    w_in = ffn_w_in.astype(BF16)
    w_out = ffn_w_out.astype(BF16)
    gdn_w = gdn_w_in.astype(BF16)
    gdn_wo = gdn_w_out.astype(BF16)
    mla_wo = mla_w_out.astype(BF16)
    k_sh = vt_sh = None
    for l in range(depth):
        m = lambda i: mod[l, :, i]
        x = _ffn(x, norm_g[l, 0], m(0), m(1), m(2), w_in, w_out, (l, 0))
        if l < n_a:
            w = gdn_w_in[l]
            w_ba = jnp.concatenate([_pad_cols(w[:, 4 * width:4 * width + g_heads], LANE),
                                    _pad_cols(w[:, 4 * width + g_heads:], LANE)], axis=1).astype(BF16)
            qkv, z, ba = _gdn_in(x, norm_g[l, 1], m(3), m(4), gdn_w, l, width, w_ba)
            y = _gdn_core(qkv, z, ba, gdn_conv_w[l], gdn_a_log[l], gdn_dt_bias[l], gdn_norm_g[l])
            pre = (y, gdn_wo, l, m(5))
        else:
            j = l - n_a
            w_uq = mla_w_uq[j].reshape(-1, m_heads, QK_NOPE + QK_ROPE)
            w_qn = w_uq[:, :, :QK_NOPE].reshape(-1, m_heads * QK_NOPE).astype(BF16)
            w_qr = jnp.pad(w_uq[:, :, QK_NOPE:], ((0, 0), (0, 0), (0, LANE - QK_ROPE)))
            w_qr = w_qr.reshape(-1, m_heads * LANE).astype(BF16)
            qg = mla_q_norm_g[j]
            q = _mla_q(x, norm_g[l, 1], m(3), m(4), mla_w_dq[j].astype(BF16),
                       mla_q_lora_norm_g[j].reshape(1, -1), w_qn, w_qr,
                       qg[:QK_NOPE].reshape(1, -1), _pad_cols(qg[QK_NOPE:].reshape(1, -1), LANE), cos, sin)
            y = _attention(q, k_sh, vt_sh)
            pre = (y, mla_wo, j, m(5))
        x = _ffn(x, norm_g[l, 2], m(6), m(7), m(8), w_in, w_out, (l, 1), pre=pre)
        if l == n_a - 1:
            w_ukv = mla_w_ukv.reshape(kv_lora, m_heads, QK_NOPE + V_HEAD)
            w_k = w_ukv[:, :, :QK_NOPE].reshape(kv_lora, m_heads * QK_NOPE).astype(BF16)
            w_v = w_ukv[:, :, QK_NOPE:].reshape(kv_lora, m_heads * V_HEAD).astype(BF16)
            kg = mla_k_norm_g
            k_sh, vt_sh = _shared_kv(
                x, kv_norm_g, kv_mod[:, 0], kv_mod[:, 1], mla_w_dkv[:, :kv_lora].astype(BF16),
                _pad_cols(mla_w_dkv[:, kv_lora:], LANE).astype(BF16), mla_kv_norm_g.reshape(1, -1), w_k, w_v,
                kg[:QK_NOPE].reshape(1, -1), _pad_cols(kg[QK_NOPE:].reshape(1, -1), LANE), cos, sin)
    return x
```

```python
import functools

import jax
import jax.numpy as jnp
from jax import lax
from jax.experimental import pallas as pl
from jax.experimental.pallas import tpu as pltpu

F32 = jnp.float32
BF16 = jnp.bfloat16

EPS = 1e-6
CHUNK = 64
ROPE_BASE = 10000.0
LANE = 128
VMEM_LIMIT = 56 * 1024 * 1024

GDN_HEAD_DIM = 128
QK_NOPE = 128
QK_ROPE = 64
V_HEAD = 128
V_AUG = V_HEAD + 16
QK_PAD = 256


def _cparams(*sem):
    return pltpu.CompilerParams(dimension_semantics=sem, vmem_limit_bytes=VMEM_LIMIT)


def _silu(t):
    half = 0.5 * t
    return half + half * jnp.tanh(half)


def _dot(a, b):
    return jnp.dot(a, b, preferred_element_type=F32)


def _dot_nt(a, b):
    return lax.dot_general(a, b, (((1,), (1,)), ((), ())), preferred_element_type=F32)


def _dot_tn(a, b):
    return lax.dot_general(a, b, (((0,), (0,)), ((), ())), preferred_element_type=F32)


def _split3(t):
    hi = t.astype(BF16)
    r1 = t - hi.astype(F32)
    mid = r1.astype(BF16)
    lo = (r1 - mid.astype(F32)).astype(BF16)
    return hi, mid, lo


def _const_spec(shape):
    nd = len(shape)
    return pl.BlockSpec(shape, lambda *_: (0,) * nd, pipeline_mode=pl.Buffered(1))


def _pick_spec(lead, block, tail=None):
    index = tuple(lead) + tuple(tail or (0,) * len(block))
    return pl.BlockSpec((None,) * len(lead) + tuple(block), lambda *_: index, pipeline_mode=pl.Buffered(1))


def _modulated(x, g, shift, scale):
    ms = jnp.mean(x * x, axis=-1, keepdims=True)
    return (x * lax.rsqrt(ms + EPS)) * (g * (1.0 + scale)) + shift


def _mod_kernel(c_ref, w_ref, b_ref, o_ref):
    ca = _silu(c_ref[...])
    c_hi = ca.astype(BF16)
    c_lo = (ca - c_hi.astype(F32)).astype(BF16)
    w = w_ref[0]
    w_hi = w.astype(BF16)
    w_lo = (w - w_hi.astype(F32)).astype(BF16)
    o_ref[0] = _dot(c_hi, w_hi) + (_dot(c_lo, w_hi) + _dot(c_hi, w_lo)) + b_ref[0]


def _modulation(c_pad, w, b):
    nl, d, n = w.shape
    tn = 1024
    return pl.pallas_call(
        _mod_kernel,
        grid=(nl, n // tn),
        in_specs=[
            pl.BlockSpec((8, d), lambda l, j: (0, 0)),
            pl.BlockSpec((1, d, tn), lambda l, j: (l, 0, j)),
            pl.BlockSpec((1, 1, tn), lambda l, j: (l, 0, j)),
        ],
        out_specs=pl.BlockSpec((1, 8, tn), lambda l, j: (l, 0, j)),
        out_shape=jax.ShapeDtypeStruct((nl, 8, n), F32),
        compiler_params=_cparams("parallel", "parallel"),
        name="adaln_mod",
    )(c_pad, w, b.reshape(nl, 1, n))


def _rope_kernel(pos_ref, cos_ref, sin_ref):
    half = QK_ROPE // 2
    pos = pos_ref[0].astype(F32)
    lane = lax.broadcasted_iota(jnp.int32, (1, LANE), 1)
    idx = jnp.where(lane < half, lane, lane - half).astype(F32)
    inv_freq = jnp.exp(idx * (-jnp.log(ROPE_BASE) / half))
    ang = pos * inv_freq
    valid = lane < QK_ROPE
    cos_ref[0] = jnp.where(valid, jnp.cos(ang), 0.0)
    sin_ref[0] = jnp.where(valid, jnp.where(lane < half, -jnp.sin(ang), jnp.sin(ang)), 0.0)


def _rope_tables(positions):
    b, s = positions.shape
    ts = 512
    out = jax.ShapeDtypeStruct((b, s, LANE), F32)
    return pl.pallas_call(
        _rope_kernel,
        grid=(b, s // ts),
        in_specs=[pl.BlockSpec((1, ts, 1), lambda i, j: (i, j, 0))],
        out_specs=[pl.BlockSpec((1, ts, LANE), lambda i, j: (i, j, 0))] * 2,
        out_shape=[out, out],
        compiler_params=_cparams("parallel", "parallel"),
        name="rope_tables",
    )(positions.reshape(b, s, 1))


def _rotate(t, cos, sin):
    half = QK_ROPE // 2
    lane = lax.broadcasted_iota(jnp.int32, t.shape, 1)
    swapped = jnp.where(lane < half, pltpu.roll(t, LANE - half, 1), pltpu.roll(t, half, 1))
    return t * cos + swapped * sin


def _ffn_kernel(*refs, pre):
    if pre:
        (x_ref, y_ref, wo_ref, gmix_ref, g_ref, sh_ref, sc_ref, gate_ref,
         wg_ref, wu_ref, wd_ref, out_ref) = refs
    else:
        x_ref, g_ref, sh_ref, sc_ref, gate_ref, wg_ref, wu_ref, wd_ref, out_ref = refs
    x = x_ref[0]
    if pre:
        x = x + gmix_ref[0] * _dot(y_ref[0], wo_ref[...])
    hb = _modulated(x, g_ref[...], sh_ref[0], sc_ref[0]).astype(BF16)
    gate = _dot(hb, wg_ref[...])
    up = _dot(hb, wu_ref[...])
    act = (_silu(gate) * up).astype(BF16)
    out_ref[0] = x + (0.5 * gate_ref[0]) * _dot(act, wd_ref[...])


def _ffn(x, norm_g, shift, scale, gate, w_in, w_out, idx, pre=None, tm=512):
    b, s, d = x.shape
    f = w_out.shape[-2]
    tile = pl.BlockSpec((1, tm, d), lambda i, j: (i, j, 0))
    vec = pl.BlockSpec((1, 1, d), lambda i, j: (i, 0, 0))
    args, specs = [x], [tile]
    if pre is not None:
        y, w_o, jo, g_mix = pre
        args += [y, w_o, g_mix]
        specs += [pl.BlockSpec((1, tm, y.shape[-1]), lambda i, j: (i, j, 0)), _pick_spec((jo,), w_o.shape[1:]), vec]
    args += [norm_g.reshape(1, d), shift, scale, gate, w_in, w_in, w_out]
    specs += [_const_spec((1, d)), vec, vec, vec,
              _pick_spec(idx, (d, f), (0, 0)), _pick_spec(idx, (d, f), (0, 1)), _pick_spec(idx, (f, d))]
    return pl.pallas_call(
        functools.partial(_ffn_kernel, pre=pre is not None),
        grid=(b, s // tm),
        in_specs=specs,
        out_specs=tile,
        out_shape=jax.ShapeDtypeStruct((b, s, d), F32),
        compiler_params=_cparams("parallel", "parallel"),
        name="ffn_pre" if pre is not None else "ffn",
    )(*args)


def _gdn_in_kernel(x_ref, g_ref, sh_ref, sc_ref, w_ref, wba_ref, qkv_ref, z_ref, ba_ref):
    hb = _modulated(x_ref[0], g_ref[...], sh_ref[0], sc_ref[0]).astype(BF16)
    width = z_ref.shape[-1]
    p = _dot(hb, w_ref[...])
    qkv_ref[0] = p[:, :3 * width]
    z_ref[0] = p[:, 3 * width:]
    ba_ref[0] = _dot(hb, wba_ref[...])


def _gdn_in(x, norm_g, shift, scale, w_all, l, width, w_ba, tm=512):
    b, s, d = x.shape
    tile = lambda n: pl.BlockSpec((1, tm, n), lambda i, j: (i, j, 0))
    vec = pl.BlockSpec((1, 1, d), lambda i, j: (i, 0, 0))
    return pl.pallas_call(
        _gdn_in_kernel,
        grid=(b, s // tm),
        in_specs=[tile(d), _const_spec((1, d)), vec, vec, _pick_spec((l,), (d, 4 * width)), _const_spec(w_ba.shape)],
        out_specs=[tile(3 * width), tile(width), tile(2 * LANE)],
        out_shape=[jax.ShapeDtypeStruct((b, s, 3 * width), F32),
                   jax.ShapeDtypeStruct((b, s, width), F32),
                   jax.ShapeDtypeStruct((b, s, 2 * LANE), F32)],
        compiler_params=_cparams("parallel", "parallel"),
        name="gdn_in",
    )(x, norm_g.reshape(1, d), shift, scale, w_all, w_ba)


def _split2(t):
    hi = t.astype(BF16)
    return hi, (t - hi.astype(F32)).astype(BF16)


def _dot2(pieces, m):
    return _dot(pieces[0], m) + _dot(pieces[1], m)


def _dot2_left(m, pieces):
    return _dot(m, pieces[0]) + _dot(m, pieces[1])


def _tile_rows(t, n):
    return jnp.concatenate([t] * n, axis=0)


def _iota2(shape):
    return lax.broadcasted_iota(jnp.int32, shape, 0), lax.broadcasted_iota(jnp.int32, shape, 1)


GDN_GROUP = 4
NEUMANN_SPLIT_STEPS = 3


def _gdn_intra_kernel(qkv_ref, ba_ref, convw_ref, alog_ref, dtb_ref,
                      u_ref, w_ref, qe_ref, kd_ref, attn_ref, egl_ref, tail_ref, *, heads, nchunk):
    c = CHUNK
    dk = GDN_HEAD_DIM
    width = heads * dk
    ts = nchunk * c
    gw = GDN_GROUP * dk
    pw_ = GDN_GROUP * c
    ngroup = heads // GDN_GROUP

    @pl.when(pl.program_id(1) == 0)
    def _():
        tail_ref[...] = jnp.zeros(tail_ref.shape, F32)

    cur = qkv_ref[0]
    prev = tail_ref[...]
    k_taps = convw_ref.shape[0]
    acc = cur * convw_ref[k_taps - 1:k_taps, :]
    cur3 = cur.reshape(ts // 8, 8, 3 * width)
    row3 = lax.broadcasted_iota(jnp.int32, (1, 8, 1), 1)
    for j in range(k_taps - 1):
        d = k_taps - 1 - j
        rot = pltpu.roll(cur3, d, 1)
        above = jnp.concatenate([pltpu.roll(prev, d, 0)[None], rot[:-1]], axis=0)
        acc = acc + jnp.where(row3 < d, above, rot).reshape(ts, 3 * width) * convw_ref[j:j + 1, :]
    tail_ref[...] = cur[ts - 8:]
    act = _silu(acc)

    def l2n(t, scale):
        parts = []
        for h in range(heads):
            th = t[:, h * dk:(h + 1) * dk]
            parts.append(th * (lax.rsqrt(jnp.sum(th * th, axis=-1, keepdims=True) + EPS) * scale))
        return jnp.concatenate(parts, axis=1)

    qn = l2n(act[:, :width], dk ** -0.5)
    kn = l2n(act[:, width:2 * width], 1.0)
    v16 = act[:, 2 * width:].astype(BF16)
    k16 = kn.astype(BF16)
    q16 = qn.astype(BF16)

    ba = ba_ref[0]
    beta = 1.0 / (1.0 + jnp.exp(-ba[:, :LANE]))
    al = ba[:, LANE:] + dtb_ref[...]
    softplus = jnp.maximum(al, 0.0) + jnp.log1p(jnp.exp(-jnp.abs(al)))
    lane = lax.broadcasted_iota(jnp.int32, (ts, LANE), 1)
    gl = jnp.where(lane < heads, -jnp.exp(alog_ref[...]) * softplus, 0.0)

    r, cc = _iota2((ts, ts))
    tri = ((r // c == cc // c) & (r >= cc)).astype(BF16)
    gcum = _dot2_left(tri, _split2(gl))
    r, cc = _iota2((LANE, width))
    e_full = (r == cc // dk).astype(BF16)
    r, cc = _iota2((LANE, heads * c))
    e_pack = (r == cc // c).astype(BF16)
    gc_pieces = _split2(gcum)
    gc_full = _dot2(gc_pieces, e_full)
    gc_pack = _dot2(gc_pieces, e_pack)
    beta_pack = _dot(beta.astype(BF16), e_pack)

    qe_ref[0] = (qn * jnp.exp(gc_full)).astype(qe_ref.dtype)

    r, cc = _iota2((c, pw_))
    eye_p = r == cc % c
    incl_p = r >= cc % c
    strict_p = r > cc % c
    r, cc = _iota2((pw_, gw))
    bd_wide = r // c == cc // dk
    r, cc = _iota2((pw_, pw_))
    bd_sq = r // c == cc // c
    zero16 = jnp.zeros((), BF16)

    units = [(ci, g) for ci in range(nchunk) for g in range(ngroup)]
    rows = lambda ci: slice(ci * c, (ci + 1) * c)
    gsl = lambda g: slice(g * gw, (g + 1) * gw)
    psl = lambda g: slice(g * pw_, (g + 1) * pw_)

    for ci in range(nchunk):
        g_last = gc_full[ci * c + c - 1:ci * c + c, :]
        egl_ref[0, ci] = jnp.exp(g_last)
        kd_ref[0, rows(ci), :] = (kn[rows(ci)] * jnp.exp(g_last - gc_full[rows(ci)])).astype(kd_ref.dtype)

    bd_k = [jnp.where(bd_wide, _tile_rows(k16[rows(ci), gsl(g)], GDN_GROUP), zero16) for ci, g in units]
    sc = [_dot_nt(jnp.concatenate([k16[rows(ci), gsl(g)], q16[rows(ci), gsl(g)]], axis=0), bd_k[i])
          for i, (ci, g) in enumerate(units)]
    a_p, beta_r, eg_r = [], [], []
    for i, (ci, g) in enumerate(units):
        gcp = gc_pack[rows(ci), psl(g)]
        gr = jnp.sum(jnp.where(eye_p, gcp, 0.0), axis=0, keepdims=True)
        bp = beta_pack[rows(ci), psl(g)]
        beta_r.append(jnp.sum(jnp.where(eye_p, bp, 0.0), axis=0, keepdims=True))
        eg_r.append(jnp.exp(gr))
        decay = jnp.where(incl_p, jnp.exp(jnp.where(incl_p, gcp - gr, 0.0)), 0.0)
        a_p.append(jnp.where(strict_p, sc[i][:c] * bp * decay, 0.0))
        attn_ref[0, rows(ci), psl(g)] = jnp.where(incl_p, sc[i][c:] * decay, 0.0).astype(attn_ref.dtype)

    def bd(t16):
        return jnp.where(bd_sq, _tile_rows(t16, GDN_GROUP), zero16)

    def prod(lhs, rhs, hi):
        lh, ll = _split2(lhs)
        rh, rl = _split2(rhs)
        bh = bd(rh)
        if not hi:
            return _dot(lh, bh)
        return _dot(lh, bh) + (_dot(ll, bh) + _dot(lh, bd(rl)))

    n_units = len(units)
    n_sq = c.bit_length() - 2
    t_inv = [jnp.where(eye_p, 1.0, 0.0) - a for a in a_p]
    pw = [prod(a_p[i], a_p[i], NEUMANN_SPLIT_STEPS > 0) for i in range(n_units)]
    for step in range(n_sq):
        hi = step < NEUMANN_SPLIT_STEPS
        if step < n_sq - 1:
            both = [prod(jnp.concatenate([t_inv[i], pw[i]], axis=0), pw[i], hi) for i in range(n_units)]
            t_inv = [t_inv[i] + both[i][:c] for i in range(n_units)]
            pw = [both[i][c:] for i in range(n_units)]
        else:
            t_inv = [t_inv[i] + prod(t_inv[i], pw[i], hi) for i in range(n_units)]

    for i, (ci, g) in enumerate(units):
        t_u = (t_inv[i] * beta_r[i]).astype(BF16)
        t_w = (t_inv[i] * (beta_r[i] * eg_r[i])).astype(BF16)
        bd_v = jnp.where(bd_wide, _tile_rows(v16[rows(ci), gsl(g)], GDN_GROUP), zero16)
        u_ref[0, rows(ci), gsl(g)] = _dot(t_u, bd_v).astype(u_ref.dtype)
        w_ref[0, rows(ci), gsl(g)] = _dot(t_w, bd_k[i]).astype(w_ref.dtype)


def _gdn_seq_kernel(u_ref, w_ref, qe_ref, kd_ref, attn_ref, egl_ref, z_ref, ng_ref, o_ref, state_ref,
                    *, heads, nchunk, nseq):
    c = CHUNK
    dk = GDN_HEAD_DIM
    gw = GDN_GROUP * dk
    pw_ = GDN_GROUP * c
    ngroup = heads // GDN_GROUP

    @pl.when(pl.program_id(1) == 0)
    def _():
        state_ref[...] = jnp.zeros_like(state_ref)

    r, cc = _iota2((pw_, gw))
    bd_wide = r // c == cc // dk
    zero16 = jnp.zeros((), BF16)
    zero = jnp.zeros((dk, dk), BF16)
    hsl = lambda h: slice(h * dk, (h + 1) * dk)
    units = [(bi, h) for bi in range(nseq) for h in range(heads)]
    for ci in range(nchunk):
        rows = slice(ci * c, (ci + 1) * c)
        st16 = {bh: state_ref[bh[0], bh[1]].astype(BF16) for bh in units}
        rd = {}
        for bi, h in units[::2]:
            psl = slice(h * dk, (h + 2) * dk)
            st_pair = jnp.concatenate([jnp.concatenate([st16[bi, h], zero], axis=1),
                                       jnp.concatenate([zero, st16[bi, h + 1]], axis=1)], axis=0)
            pair = _dot(jnp.concatenate([w_ref[bi, rows, psl], qe_ref[bi, rows, psl]], axis=0), st_pair)
            rd[bi, h], rd[bi, h + 1] = pair[:, :dk], pair[:, dk:]
        vn16 = {(bi, h): (u_ref[bi, rows, hsl(h)] - rd[bi, h][:c]).astype(BF16) for bi, h in units}
        intra = {}
        for bi in range(nseq):
            for g in range(ngroup):
                vg = jnp.concatenate([vn16[bi, h] for h in range(g * GDN_GROUP, (g + 1) * GDN_GROUP)], axis=1)
                bd_vn = jnp.where(bd_wide, _tile_rows(vg, GDN_GROUP), zero16)
                intra[bi, g] = _dot(attn_ref[bi, rows, g * pw_:(g + 1) * pw_], bd_vn)
        for bi, h in units:
            state_ref[bi, h] = (state_ref[bi, h] * egl_ref[bi, ci, :, hsl(h)]
                                + _dot_tn(kd_ref[bi, rows, hsl(h)], vn16[bi, h]))
        for bi, h in units:
            g, hh = divmod(h, GDN_GROUP)
            o = rd[bi, h][c:] + intra[bi, g][:, hh * dk:(hh + 1) * dk]
            o = o * lax.rsqrt(jnp.mean(o * o, axis=-1, keepdims=True) + EPS) * ng_ref[...]
            o_ref[bi, rows, hsl(h)] = (o * _silu(z_ref[bi, rows, hsl(h)])).astype(o_ref.dtype)


def _gdn_core(qkv, z, ba, conv_w, a_log, dt_bias, norm_g, nchunk=4):
    b, s, width = z.shape
    heads = width // GDN_HEAD_DIM
    ts = nchunk * CHUNK
    pad = lambda t: jnp.pad(t.reshape(1, heads), ((0, 0), (0, LANE - heads)))
    tile = lambda n: pl.BlockSpec((1, ts, n), lambda i, j: (i, j, 0))
    egl_spec = pl.BlockSpec((1, nchunk, 1, width), lambda i, j: (i, j, 0, 0))
    packed = heads * CHUNK
    u, w, qe, kd, attn, egl = pl.pallas_call(
        functools.partial(_gdn_intra_kernel, heads=heads, nchunk=nchunk),
        grid=(b, s // ts),
        in_specs=[tile(3 * width), tile(2 * LANE), _const_spec(conv_w.shape),
                  _const_spec((1, LANE)), _const_spec((1, LANE))],
        out_specs=[tile(width), tile(width), tile(width), tile(width), tile(packed), egl_spec],
        out_shape=[jax.ShapeDtypeStruct((b, s, width), F32),
                   jax.ShapeDtypeStruct((b, s, width), BF16),
                   jax.ShapeDtypeStruct((b, s, width), BF16),
                   jax.ShapeDtypeStruct((b, s, width), BF16),
                   jax.ShapeDtypeStruct((b, s, packed), BF16),
                   jax.ShapeDtypeStruct((b, s // CHUNK, 1, width), F32)],
        scratch_shapes=[pltpu.VMEM((8, 3 * width), F32)],
        compiler_params=_cparams("parallel", "arbitrary"),
        name="gdn_intra",
    )(qkv, ba, conv_w, pad(a_log), pad(dt_bias))
    nseq = 2 if b % 2 == 0 else 1
    seq_tile = lambda n: pl.BlockSpec((nseq, ts, n), lambda i, j: (i, j, 0))
    return pl.pallas_call(
        functools.partial(_gdn_seq_kernel, heads=heads, nchunk=nchunk, nseq=nseq),
        grid=(b // nseq, s // ts),
        in_specs=[seq_tile(width), seq_tile(width), seq_tile(width), seq_tile(width), seq_tile(packed),
                  pl.BlockSpec((nseq, nchunk, 1, width), lambda i, j: (i, j, 0, 0)),
                  seq_tile(width), _const_spec((1, GDN_HEAD_DIM))],
        out_specs=seq_tile(width),
        out_shape=jax.ShapeDtypeStruct((b, s, width), BF16),
        scratch_shapes=[pltpu.VMEM((nseq, heads, GDN_HEAD_DIM, GDN_HEAD_DIM), F32)],
        compiler_params=_cparams("parallel", "arbitrary"),
        name="gdn_seq",
    )(u, w, qe, kd, attn, egl, z, norm_g.reshape(1, GDN_HEAD_DIM))


def _kv_kernel(x_ref, g_ref, sh_ref, sc_ref, wdl_ref, wdr_ref, lg_ref, wk_ref, wv_ref,
               kgn_ref, kgr_ref, cos_ref, sin_ref, k_ref, vt_ref, *, heads):
    hb = _modulated(x_ref[0], g_ref[...], sh_ref[0], sc_ref[0]).astype(BF16)
    lat = _dot(hb, wdl_ref[...])
    rope = _dot(hb, wdr_ref[...])
    lat = lat * lax.rsqrt(jnp.mean(lat * lat, axis=-1, keepdims=True) + EPS) * lg_ref[...]
    lat16 = lat.astype(BF16)
    k_nope = _dot(lat16, wk_ref[...])
    v = _dot(lat16, wv_ref[...])
    rope_sq = jnp.sum(rope * rope, axis=-1, keepdims=True)
    for h in range(heads):
        sl = slice(h * QK_NOPE, (h + 1) * QK_NOPE)
        kn = k_nope[:, sl]
        inv = lax.rsqrt((jnp.sum(kn * kn, axis=-1, keepdims=True) + rope_sq) / (QK_NOPE + QK_ROPE) + EPS)
        k_ref[0, h, :, :QK_NOPE] = (kn * inv * kgn_ref[...]).astype(k_ref.dtype)
        kr = _rotate(rope * inv * kgr_ref[...], cos_ref[0], sin_ref[0])
        k_ref[0, h, :, QK_NOPE:] = kr.astype(k_ref.dtype)
        vt_ref[0, h, 0, :V_HEAD, :] = jnp.transpose(v[:, h * V_HEAD:(h + 1) * V_HEAD]).astype(vt_ref.dtype)
        vt_ref[0, h, 0, V_HEAD:, :] = jnp.ones((V_AUG - V_HEAD, vt_ref.shape[-1]), vt_ref.dtype)


def _shared_kv(x, norm_g, shift, scale, w_dl, w_dr, lat_g, w_k, w_v, kg_nope, kg_rope, cos, sin, tm=512):
    b, s, d = x.shape
    heads = w_k.shape[1] // QK_NOPE
    tile = lambda n: pl.BlockSpec((1, tm, n), lambda i, j: (i, j, 0))
    vec = pl.BlockSpec((1, 1, d), lambda i, j: (i, 0, 0))
    return pl.pallas_call(
        functools.partial(_kv_kernel, heads=heads),
        grid=(b, s // tm),
        in_specs=[tile(d), _const_spec((1, d)), vec, vec, _const_spec(w_dl.shape), _const_spec(w_dr.shape),
                  _const_spec(lat_g.shape), _const_spec(w_k.shape), _const_spec(w_v.shape),
                  _const_spec(kg_nope.shape), _const_spec(kg_rope.shape), tile(LANE), tile(LANE)],
        out_specs=[pl.BlockSpec((1, heads, tm, QK_PAD), lambda i, j: (i, 0, j, 0)),
                   pl.BlockSpec((1, heads, 1, V_AUG, tm), lambda i, j: (i, 0, j, 0, 0))],
        out_shape=[jax.ShapeDtypeStruct((b, heads, s, QK_PAD), BF16),
                   jax.ShapeDtypeStruct((b, heads, s // tm, V_AUG, tm), BF16)],
        compiler_params=_cparams("parallel", "parallel"),
        name="mla_kv",
    )(x, norm_g.reshape(1, d), shift, scale, w_dl, w_dr, lat_g, w_k, w_v, kg_nope, kg_rope, cos, sin)


def _q_kernel(x_ref, g_ref, sh_ref, sc_ref, wdq_ref, qlg_ref, wqn_ref, wqr_ref,
              qgn_ref, qgr_ref, cos_ref, sin_ref, q_ref, *, heads):
    hb = _modulated(x_ref[0], g_ref[...], sh_ref[0], sc_ref[0]).astype(BF16)
    ql = _dot(hb, wdq_ref[...])
    ql = ql * lax.rsqrt(jnp.mean(ql * ql, axis=-1, keepdims=True) + EPS) * qlg_ref[...]
    ql16 = ql.astype(BF16)
    q_nope = _dot(ql16, wqn_ref[...])
    q_rope = _dot(ql16, wqr_ref[...])
    sm_scale = (QK_NOPE + QK_ROPE) ** -0.5 * 1.4426950408889634
    qn = [q_nope[:, h * LANE:(h + 1) * LANE] for h in range(heads)]
    qr = [q_rope[:, h * LANE:(h + 1) * LANE] for h in range(heads)]
    ssq = [jnp.sum(a * a, axis=-1, keepdims=True) + jnp.sum(r * r, axis=-1, keepdims=True) for a, r in zip(qn, qr)]
    inv = [lax.rsqrt(t / (QK_NOPE + QK_ROPE) + EPS) * sm_scale for t in ssq]
    for h in range(heads):
        q_ref[0, h, :, :QK_NOPE] = (qn[h] * inv[h] * qgn_ref[...]).astype(q_ref.dtype)
    for h in range(heads):
        q_ref[0, h, :, QK_NOPE:] = _rotate(qr[h] * inv[h] * qgr_ref[...], cos_ref[0], sin_ref[0]).astype(q_ref.dtype)


def _mla_q(x, norm_g, shift, scale, w_dq, ql_g, w_qn, w_qr, qg_nope, qg_rope, cos, sin, tm=512):
    b, s, d = x.shape
    heads = w_qn.shape[1] // QK_NOPE
    tile = lambda n: pl.BlockSpec((1, tm, n), lambda i, j: (i, j, 0))
    vec = pl.BlockSpec((1, 1, d), lambda i, j: (i, 0, 0))
    return pl.pallas_call(
        functools.partial(_q_kernel, heads=heads),
        grid=(b, s // tm),
        in_specs=[tile(d), _const_spec((1, d)), vec, vec, _const_spec(w_dq.shape), _const_spec(ql_g.shape),
                  _const_spec(w_qn.shape), _const_spec(w_qr.shape), _const_spec(qg_nope.shape),
                  _const_spec(qg_rope.shape), tile(LANE), tile(LANE)],
        out_specs=pl.BlockSpec((1, heads, tm, QK_PAD), lambda i, j: (i, 0, j, 0)),
        out_shape=jax.ShapeDtypeStruct((b, heads, s, QK_PAD), BF16),
        compiler_params=_cparams("parallel", "parallel"),
        name="mla_q",
    )(x, norm_g.reshape(1, d), shift, scale, w_dq, ql_g, w_qn, w_qr, qg_nope, qg_rope, cos, sin)


def _attn_kernel(q_ref, k_ref, vt_ref, o_ref, s_ref, p_ref, m_ref, al_ref, acc_ref, *, tq):
    nq = q_ref.shape[2] // tq
    pairs = [(qi, j) for qi in range(nq) for j in range(qi + 1)]

    def scores(slot, qi, j):
        s_ref[slot] = _dot_nt(k_ref[0, 0, j * tq:(j + 1) * tq, :], q_ref[0, 0, qi * tq:(qi + 1) * tq, :])

    def softmax(slot, qi, j):
        s = s_ref[slot]
        if j == qi:
            key_c = lax.broadcasted_iota(jnp.int32, (tq, tq), 0) // CHUNK
            qry_c = lax.broadcasted_iota(jnp.int32, (tq, tq), 1) // CHUNK
            s = jnp.where(key_c <= qry_c, s, -jnp.inf)
        m_new = jnp.max(s, axis=0, keepdims=True)
        if j > 0:
            m = m_ref[...]
            m_new = jnp.maximum(m, m_new)
            al_ref[slot] = jnp.exp2(m - m_new)
        p_ref[slot] = jnp.exp2(s - m_new).astype(BF16)
        if j < qi:
            m_ref[...] = m_new

    def value(slot, qi, j):
        acc = _dot(vt_ref[0, 0, j], p_ref[slot])
        if j > 0:
            acc = al_ref[slot] * acc_ref[...] + acc
        if j == qi:
            out = acc[:V_HEAD] / acc[V_HEAD:V_HEAD + 1]
            o_ref[0, qi * tq:(qi + 1) * tq, :] = jnp.transpose(out).astype(o_ref.dtype)
        else:
            acc_ref[...] = acc

    scores(0, *pairs[0])
    for t, pair in enumerate(pairs):
        if t + 1 < len(pairs):
            scores((t + 1) % 2, *pairs[t + 1])
        if t > 0:
            value((t - 1) % 2, *pairs[t - 1])
        softmax(t % 2, *pair)
    value((len(pairs) - 1) % 2, *pairs[-1])


def _attention(q, k, vt):
    b, heads, s, dq = q.shape
    tq = vt.shape[-1]
    nkb = s // tq
    whole = lambda i, h: (i, h, 0, 0)
    return pl.pallas_call(
        functools.partial(_attn_kernel, tq=tq),
        grid=(b, heads),
        in_specs=[pl.BlockSpec((1, 1, s, dq), whole),
                  pl.BlockSpec((1, 1, s, dq), whole),
                  pl.BlockSpec((1, 1, nkb, V_AUG, tq), lambda i, h: (i, h, 0, 0, 0))],
        out_specs=pl.BlockSpec((1, s, V_HEAD), lambda i, h: (i, 0, h)),
        out_shape=jax.ShapeDtypeStruct((b, s, heads * V_HEAD), BF16),
        scratch_shapes=[pltpu.VMEM((2, tq, tq), F32), pltpu.VMEM((2, tq, tq), BF16), pltpu.VMEM((1, tq), F32),
                        pltpu.VMEM((2, 1, tq), F32), pltpu.VMEM((V_AUG, tq), F32)],
        compiler_params=_cparams("parallel", "parallel"),
        name="mla_attn",
    )(q, k, vt)


def _pad_cols(w, n):
    return jnp.pad(w, ((0, 0), (0, n - w.shape[1])))


def kernel(x, c, positions, ada_w, ada_b, norm_g, ffn_w_in, ffn_w_out, gdn_w_in, gdn_conv_w, gdn_a_log,
           gdn_dt_bias, gdn_norm_g, gdn_w_out, kv_ada_w, kv_ada_b, kv_norm_g, mla_w_dkv, mla_kv_norm_g,
           mla_w_ukv, mla_k_norm_g, mla_w_dq, mla_q_lora_norm_g, mla_w_uq, mla_q_norm_g, mla_w_out):
    b, s, d = x.shape
    depth = ada_w.shape[0]
    n_a = gdn_w_in.shape[0]
    n_mod = ada_w.shape[2] // d
    width = gdn_w_out.shape[1]
    g_heads = width // GDN_HEAD_DIM
    kv_lora = mla_kv_norm_g.shape[0]
    m_heads = mla_w_ukv.shape[1] // (QK_NOPE + V_HEAD)

    c_pad = jnp.pad(c, ((0, 8 - b), (0, 0)))
    mod = _modulation(c_pad, ada_w, ada_b)[:, :b].reshape(depth, b, n_mod, 1, d)
    kv_mod = _modulation(c_pad, kv_ada_w[None], kv_ada_b[None])[0, :b].reshape(b, 2, 1, d)
    cos, sin = _rope_tables(positions)

    w_in = ffn_w_in.astype(BF16)
    w_out = ffn_w_out.astype(BF16)
    gdn_w = gdn_w_in.astype(BF16)
    gdn_wo = gdn_w_out.astype(BF16)
    mla_wo = mla_w_out.astype(BF16)
    k_sh = vt_sh = None
    for l in range(depth):
        m = lambda i: mod[l, :, i]
        x = _ffn(x, norm_g[l, 0], m(0), m(1), m(2), w_in, w_out, (l, 0))
        if l < n_a:
            w = gdn_w_in[l]
            w_ba = jnp.concatenate([_pad_cols(w[:, 4 * width:4 * width + g_heads], LANE),
                                    _pad_cols(w[:, 4 * width + g_heads:], LANE)], axis=1).astype(BF16)
            qkv, z, ba = _gdn_in(x, norm_g[l, 1], m(3), m(4), gdn_w, l, width, w_ba)
            y = _gdn_core(qkv, z, ba, gdn_conv_w[l], gdn_a_log[l], gdn_dt_bias[l], gdn_norm_g[l])
            pre = (y, gdn_wo, l, m(5))
        else:
            j = l - n_a
            w_uq = mla_w_uq[j].reshape(-1, m_heads, QK_NOPE + QK_ROPE)
            w_qn = w_uq[:, :, :QK_NOPE].reshape(-1, m_heads * QK_NOPE).astype(BF16)
            w_qr = jnp.pad(w_uq[:, :, QK_NOPE:], ((0, 0), (0, 0), (0, LANE - QK_ROPE)))
            w_qr = w_qr.reshape(-1, m_heads * LANE).astype(BF16)
            qg = mla_q_norm_g[j]
            q = _mla_q(x, norm_g[l, 1], m(3), m(4), mla_w_dq[j].astype(BF16),
                       mla_q_lora_norm_g[j].reshape(1, -1), w_qn, w_qr,
                       qg[:QK_NOPE].reshape(1, -1), _pad_cols(qg[QK_NOPE:].reshape(1, -1), LANE), cos, sin)
            y = _attention(q, k_sh, vt_sh)
            pre = (y, mla_wo, j, m(5))
        x = _ffn(x, norm_g[l, 2], m(6), m(7), m(8), w_in, w_out, (l, 1), pre=pre)
        if l == n_a - 1:
            w_ukv = mla_w_ukv.reshape(kv_lora, m_heads, QK_NOPE + V_HEAD)
            w_k = w_ukv[:, :, :QK_NOPE].reshape(kv_lora, m_heads * QK_NOPE).astype(BF16)
            w_v = w_ukv[:, :, QK_NOPE:].reshape(kv_lora, m_heads * V_HEAD).astype(BF16)
            kg = mla_k_norm_g
            k_sh, vt_sh = _shared_kv(
                x, kv_norm_g, kv_mod[:, 0], kv_mod[:, 1], mla_w_dkv[:, :kv_lora].astype(BF16),
                _pad_cols(mla_w_dkv[:, kv_lora:], LANE).astype(BF16), mla_kv_norm_g.reshape(1, -1), w_k, w_v,
                kg[:QK_NOPE].reshape(1, -1), _pad_cols(kg[QK_NOPE:].reshape(1, -1), LANE), cos, sin)
    return x
```

```python
import functools

import jax
import jax.numpy as jnp
from jax import lax
from jax.experimental import pallas as pl
from jax.experimental.pallas import tpu as pltpu

F32 = jnp.float32
BF16 = jnp.bfloat16

EPS = 1e-6
CHUNK = 64
ROPE_BASE = 10000.0
LANE = 128
VMEM_LIMIT = 56 * 1024 * 1024

GDN_HEAD_DIM = 128
QK_NOPE = 128
QK_ROPE = 64
V_HEAD = 128
V_AUG = V_HEAD + 16
QK_PAD = 256


def _cparams(*sem):
    return pltpu.CompilerParams(dimension_semantics=sem, vmem_limit_bytes=VMEM_LIMIT)


def _silu(t):
    half = 0.5 * t
    return half + half * jnp.tanh(half)


def _dot(a, b):
    return jnp.dot(a, b, preferred_element_type=F32)


def _dot_nt(a, b):
    return lax.dot_general(a, b, (((1,), (1,)), ((), ())), preferred_element_type=F32)


def _dot_tn(a, b):
    return lax.dot_general(a, b, (((0,), (0,)), ((), ())), preferred_element_type=F32)


def _split3(t):
    hi = t.astype(BF16)
    r1 = t - hi.astype(F32)
    mid = r1.astype(BF16)
    lo = (r1 - mid.astype(F32)).astype(BF16)
    return hi, mid, lo


def _const_spec(shape):
    nd = len(shape)
    return pl.BlockSpec(shape, lambda *_: (0,) * nd, pipeline_mode=pl.Buffered(1))


def _pick_spec(lead, block, tail=None):
    index = tuple(lead) + tuple(tail or (0,) * len(block))
    return pl.BlockSpec((None,) * len(lead) + tuple(block), lambda *_: index, pipeline_mode=pl.Buffered(1))


def _modulated(x, g, shift, scale):
    ms = jnp.mean(x * x, axis=-1, keepdims=True)
    return (x * lax.rsqrt(ms + EPS)) * (g * (1.0 + scale)) + shift


def _mod_kernel(c_ref, w_ref, b_ref, o_ref):
    ca = _silu(c_ref[...])
    c_hi = ca.astype(BF16)
    c_lo = (ca - c_hi.astype(F32)).astype(BF16)
    w = w_ref[0]
    w_hi = w.astype(BF16)
    w_lo = (w - w_hi.astype(F32)).astype(BF16)
    o_ref[0] = _dot(c_hi, w_hi) + (_dot(c_lo, w_hi) + _dot(c_hi, w_lo)) + b_ref[0]


def _modulation(c_pad, w, b):
    nl, d, n = w.shape
    tn = 1024
    return pl.pallas_call(
        _mod_kernel,
        grid=(nl, n // tn),
        in_specs=[
            pl.BlockSpec((8, d), lambda l, j: (0, 0)),
            pl.BlockSpec((1, d, tn), lambda l, j: (l, 0, j)),
            pl.BlockSpec((1, 1, tn), lambda l, j: (l, 0, j)),
        ],
        out_specs=pl.BlockSpec((1, 8, tn), lambda l, j: (l, 0, j)),
        out_shape=jax.ShapeDtypeStruct((nl, 8, n), F32),
        compiler_params=_cparams("parallel", "parallel"),
        name="adaln_mod",
    )(c_pad, w, b.reshape(nl, 1, n))


def _rope_kernel(pos_ref, cos_ref, sin_ref):
    half = QK_ROPE // 2
    pos = pos_ref[0].astype(F32)
    lane = lax.broadcasted_iota(jnp.int32, (1, LANE), 1)
    idx = jnp.where(lane < half, lane, lane - half).astype(F32)
    inv_freq = jnp.exp(idx * (-jnp.log(ROPE_BASE) / half))
    ang = pos * inv_freq
    valid = lane < QK_ROPE
    cos_ref[0] = jnp.where(valid, jnp.cos(ang), 0.0)
    sin_ref[0] = jnp.where(valid, jnp.where(lane < half, -jnp.sin(ang), jnp.sin(ang)), 0.0)


def _rope_tables(positions):
    b, s = positions.shape
    ts = 512
    out = jax.ShapeDtypeStruct((b, s, LANE), F32)
    return pl.pallas_call(
        _rope_kernel,
        grid=(b, s // ts),
        in_specs=[pl.BlockSpec((1, ts, 1), lambda i, j: (i, j, 0))],
        out_specs=[pl.BlockSpec((1, ts, LANE), lambda i, j: (i, j, 0))] * 2,
        out_shape=[out, out],
        compiler_params=_cparams("parallel", "parallel"),
        name="rope_tables",
    )(positions.reshape(b, s, 1))


def _rotate(t, cos, sin):
    half = QK_ROPE // 2
    lane = lax.broadcasted_iota(jnp.int32, t.shape, 1)
    swapped = jnp.where(lane < half, pltpu.roll(t, LANE - half, 1), pltpu.roll(t, half, 1))
    return t * cos + swapped * sin


def _ffn_kernel(*refs, pre, nxt):
    if nxt:
        *refs, wn_in_ref, wn_out_ref, out_ref, wn_in_bf_ref, wn_out_bf_ref = refs
        wn_in_bf_ref[...] = wn_in_ref[...].astype(BF16)
        wn_out_bf_ref[...] = wn_out_ref[...].astype(BF16)
    else:
        *refs, out_ref = refs
    if pre:
        x_ref, y_ref, wo_ref, gmix_ref, g_ref, sh_ref, sc_ref, gate_ref, wg_ref, wu_ref, wd_ref = refs
    else:
        x_ref, g_ref, sh_ref, sc_ref, gate_ref, wg_ref, wu_ref, wd_ref = refs
    x = x_ref[0]
    if pre:
        x = x + gmix_ref[0] * _dot(y_ref[0], wo_ref[...])
    hb = _modulated(x, g_ref[...], sh_ref[0], sc_ref[0]).astype(BF16)
    gate = _dot(hb, wg_ref[...])
    up = _dot(hb, wu_ref[...])
    act = (_silu(gate) * up).astype(BF16)
    out_ref[0] = x + (0.5 * gate_ref[0]) * _dot(act, wd_ref[...])


BF16_ROWS = 16


def _ffn(x, norm_g, shift, scale, gate, w_in, w_out, nxt=None, pre=None, tm=512):
    b, s, d = x.shape
    f = w_out.shape[0]
    nj = s // tm
    tile = pl.BlockSpec((1, tm, d), lambda i, j: (i, j, 0))
    vec = pl.BlockSpec((1, 1, d), lambda i, j: (i, 0, 0))
    args, specs = [x], [tile]
    if pre is not None:
        y, w_o, jo, g_mix = pre
        args += [y, w_o, g_mix]
        specs += [pl.BlockSpec((1, tm, y.shape[-1]), lambda i, j: (i, j, 0)), _pick_spec((jo,), w_o.shape[1:]), vec]
    args += [norm_g.reshape(1, d), shift, scale, gate, w_in, w_in, w_out]
    specs += [_const_spec((1, d)), vec, vec, vec,
              _pick_spec((), (d, f), (0, 0)), _pick_spec((), (d, f), (0, 1)), _pick_spec((), (f, d))]
    out_specs, out_shape = [tile], [jax.ShapeDtypeStruct((b, s, d), F32)]
    if nxt is not None:
        w_in_all, w_out_all, (ln, kn) = nxt
        steps = b * nj
        rows_in = d // steps
        cols_out = f * d // (BF16_ROWS * steps)
        assert rows_in % BF16_ROWS == 0 and rows_in * steps == d
        assert cols_out % LANE == 0 and cols_out * BF16_ROWS * steps == f * d
        w_out_view = w_out_all.reshape(w_out_all.shape[:2] + (BF16_ROWS * steps, cols_out))
        args += [w_in_all, w_out_view]
        specs += [pl.BlockSpec((None, None, rows_in, 2 * f), lambda i, j: (ln, kn, i * nj + j, 0)),
                  pl.BlockSpec((None, None, BF16_ROWS, cols_out), lambda i, j: (ln, kn, i * nj + j, 0))]
        out_specs += [pl.BlockSpec((rows_in, 2 * f), lambda i, j: (i * nj + j, 0)),
                      pl.BlockSpec((BF16_ROWS, cols_out), lambda i, j: (i * nj + j, 0))]
        out_shape += [jax.ShapeDtypeStruct((d, 2 * f), BF16),
                      jax.ShapeDtypeStruct((BF16_ROWS * steps, cols_out), BF16)]
    res = pl.pallas_call(
        functools.partial(_ffn_kernel, pre=pre is not None, nxt=nxt is not None),
        grid=(b, nj),
        in_specs=specs,
        out_specs=out_specs,
        out_shape=out_shape,
        compiler_params=_cparams("parallel", "parallel"),
        name="ffn_pre" if pre is not None else "ffn",
    )(*args)
    if nxt is None:
        return res[0], None
    return res[0], (res[1], res[2].reshape(f, d))


def _gdn_in_kernel(x_ref, g_ref, sh_ref, sc_ref, w_ref, wba_ref, qkv_ref, z_ref, ba_ref):
    hb = _modulated(x_ref[0], g_ref[...], sh_ref[0], sc_ref[0]).astype(BF16)
    width = z_ref.shape[-1]
    p = _dot(hb, w_ref[...])
    qkv_ref[0] = p[:, :3 * width]
    z_ref[0] = p[:, 3 * width:]
    ba_ref[0] = _dot(hb, wba_ref[...])


def _gdn_in(x, norm_g, shift, scale, w_all, l, width, w_ba, tm=512):
    b, s, d = x.shape
    tile = lambda n: pl.BlockSpec((1, tm, n), lambda i, j: (i, j, 0))
    vec = pl.BlockSpec((1, 1, d), lambda i, j: (i, 0, 0))
    return pl.pallas_call(
        _gdn_in_kernel,
        grid=(b, s // tm),
        in_specs=[tile(d), _const_spec((1, d)), vec, vec, _pick_spec((l,), (d, 4 * width)), _const_spec(w_ba.shape)],
        out_specs=[tile(3 * width), tile(width), tile(2 * LANE)],
        out_shape=[jax.ShapeDtypeStruct((b, s, 3 * width), F32),
                   jax.ShapeDtypeStruct((b, s, width), F32),
                   jax.ShapeDtypeStruct((b, s, 2 * LANE), F32)],
        compiler_params=_cparams("parallel", "parallel"),
        name="gdn_in",
    )(x, norm_g.reshape(1, d), shift, scale, w_all, w_ba)


def _split2(t):
    hi = t.astype(BF16)
    return hi, (t - hi.astype(F32)).astype(BF16)


def _dot2(pieces, m):
    return _dot(pieces[0], m) + _dot(pieces[1], m)


def _dot2_left(m, pieces):
    return _dot(m, pieces[0]) + _dot(m, pieces[1])


def _tile_rows(t, n):
    return jnp.concatenate([t] * n, axis=0)


def _iota2(shape):
    return lax.broadcasted_iota(jnp.int32, shape, 0), lax.broadcasted_iota(jnp.int32, shape, 1)


GDN_GROUP = 4
NEUMANN_SPLIT_STEPS = 3


def _gdn_intra_kernel(qkv_ref, ba_ref, convw_ref, alog_ref, dtb_ref,
                      u_ref, w_ref, qe_ref, kd_ref, attn_ref, egl_ref, tail_ref, *, heads, nchunk):
    c = CHUNK
    dk = GDN_HEAD_DIM
    width = heads * dk
    ts = nchunk * c
    gw = GDN_GROUP * dk
    pw_ = GDN_GROUP * c
    ngroup = heads // GDN_GROUP

    @pl.when(pl.program_id(1) == 0)
    def _():
        tail_ref[...] = jnp.zeros(tail_ref.shape, F32)

    cur = qkv_ref[0]
    prev = tail_ref[...]
    k_taps = convw_ref.shape[0]
    acc = cur * convw_ref[k_taps - 1:k_taps, :]
    cur3 = cur.reshape(ts // 8, 8, 3 * width)
    row3 = lax.broadcasted_iota(jnp.int32, (1, 8, 1), 1)
    for j in range(k_taps - 1):
        d = k_taps - 1 - j
        rot = pltpu.roll(cur3, d, 1)
        above = jnp.concatenate([pltpu.roll(prev, d, 0)[None], rot[:-1]], axis=0)
        acc = acc + jnp.where(row3 < d, above, rot).reshape(ts, 3 * width) * convw_ref[j:j + 1, :]
    tail_ref[...] = cur[ts - 8:]
    act = _silu(acc)

    def l2n(t, scale):
        parts = []
        for h in range(heads):
            th = t[:, h * dk:(h + 1) * dk]
            parts.append(th * (lax.rsqrt(jnp.sum(th * th, axis=-1, keepdims=True) + EPS) * scale))
        return jnp.concatenate(parts, axis=1)

    qn = l2n(act[:, :width], dk ** -0.5)
    kn = l2n(act[:, width:2 * width], 1.0)
    v16 = act[:, 2 * width:].astype(BF16)
    k16 = kn.astype(BF16)
    q16 = qn.astype(BF16)

    ba = ba_ref[0]
    beta = 1.0 / (1.0 + jnp.exp(-ba[:, :LANE]))
    al = ba[:, LANE:] + dtb_ref[...]
    softplus = jnp.maximum(al, 0.0) + jnp.log1p(jnp.exp(-jnp.abs(al)))
    lane = lax.broadcasted_iota(jnp.int32, (ts, LANE), 1)
    gl = jnp.where(lane < heads, -jnp.exp(alog_ref[...]) * softplus, 0.0)

    r, cc = _iota2((ts, ts))
    tri = ((r // c == cc // c) & (r >= cc)).astype(BF16)
    gcum = _dot2_left(tri, _split2(gl))
    r, cc = _iota2((LANE, width))
    e_full = (r == cc // dk).astype(BF16)
    r, cc = _iota2((LANE, heads * c))
    e_pack = (r == cc // c).astype(BF16)
    gc_pieces = _split2(gcum)
    gc_full = _dot2(gc_pieces, e_full)
    gc_pack = _dot2(gc_pieces, e_pack)
    beta_pack = _dot(beta.astype(BF16), e_pack)

    qe_ref[0] = (qn * jnp.exp(gc_full)).astype(qe_ref.dtype)

    r, cc = _iota2((c, pw_))
    eye_p = r == cc % c
    incl_p = r >= cc % c
    strict_p = r > cc % c
    r, cc = _iota2((pw_, gw))
    bd_wide = r // c == cc // dk
    r, cc = _iota2((pw_, pw_))
    bd_sq = r // c == cc // c
    zero16 = jnp.zeros((), BF16)

    units = [(ci, g) for ci in range(nchunk) for g in range(ngroup)]
    rows = lambda ci: slice(ci * c, (ci + 1) * c)
    gsl = lambda g: slice(g * gw, (g + 1) * gw)
    psl = lambda g: slice(g * pw_, (g + 1) * pw_)

    for ci in range(nchunk):
        g_last = gc_full[ci * c + c - 1:ci * c + c, :]
        egl_ref[0, ci] = jnp.exp(g_last)
        kd_ref[0, rows(ci), :] = (kn[rows(ci)] * jnp.exp(g_last - gc_full[rows(ci)])).astype(kd_ref.dtype)

    bd_k = [jnp.where(bd_wide, _tile_rows(k16[rows(ci), gsl(g)], GDN_GROUP), zero16) for ci, g in units]
    sc = [_dot_nt(jnp.concatenate([k16[rows(ci), gsl(g)], q16[rows(ci), gsl(g)]], axis=0), bd_k[i])
          for i, (ci, g) in enumerate(units)]
    a_p, beta_r, eg_r = [], [], []
    for i, (ci, g) in enumerate(units):
        gcp = gc_pack[rows(ci), psl(g)]
        gr = jnp.sum(jnp.where(eye_p, gcp, 0.0), axis=0, keepdims=True)
        bp = beta_pack[rows(ci), psl(g)]
        beta_r.append(jnp.sum(jnp.where(eye_p, bp, 0.0), axis=0, keepdims=True))
        eg_r.append(jnp.exp(gr))
        decay = jnp.where(incl_p, jnp.exp(jnp.where(incl_p, gcp - gr, 0.0)), 0.0)
        a_p.append(jnp.where(strict_p, sc[i][:c] * bp * decay, 0.0))
        attn_ref[0, rows(ci), psl(g)] = jnp.where(incl_p, sc[i][c:] * decay, 0.0).astype(attn_ref.dtype)

    def bd(t16):
        return jnp.where(bd_sq, _tile_rows(t16, GDN_GROUP), zero16)

    def prod(lhs, rhs, hi):
        lh, ll = _split2(lhs)
        rh, rl = _split2(rhs)
        bh = bd(rh)
        if not hi:
            return _dot(lh, bh)
        return _dot(lh, bh) + (_dot(ll, bh) + _dot(lh, bd(rl)))

    n_units = len(units)
    n_sq = c.bit_length() - 2
    t_inv = [jnp.where(eye_p, 1.0, 0.0) - a for a in a_p]
    pw = [prod(a_p[i], a_p[i], NEUMANN_SPLIT_STEPS > 0) for i in range(n_units)]
    for step in range(n_sq):
        hi = step < NEUMANN_SPLIT_STEPS
        if step < n_sq - 1:
            both = [prod(jnp.concatenate([t_inv[i], pw[i]], axis=0), pw[i], hi) for i in range(n_units)]
            t_inv = [t_inv[i] + both[i][:c] for i in range(n_units)]
            pw = [both[i][c:] for i in range(n_units)]
        else:
            t_inv = [t_inv[i] + prod(t_inv[i], pw[i], hi) for i in range(n_units)]

    for i, (ci, g) in enumerate(units):
        t_u = (t_inv[i] * beta_r[i]).astype(BF16)
        t_w = (t_inv[i] * (beta_r[i] * eg_r[i])).astype(BF16)
        bd_v = jnp.where(bd_wide, _tile_rows(v16[rows(ci), gsl(g)], GDN_GROUP), zero16)
        u_ref[0, rows(ci), gsl(g)] = _dot(t_u, bd_v).astype(u_ref.dtype)
        w_ref[0, rows(ci), gsl(g)] = _dot(t_w, bd_k[i]).astype(w_ref.dtype)


def _gdn_seq_kernel(u_ref, w_ref, qe_ref, kd_ref, attn_ref, egl_ref, z_ref, ng_ref, o_ref, state_ref,
                    *, heads, nchunk, nseq):
    c = CHUNK
    dk = GDN_HEAD_DIM
    gw = GDN_GROUP * dk
    pw_ = GDN_GROUP * c
    ngroup = heads // GDN_GROUP

    @pl.when(pl.program_id(1) == 0)
    def _():
        state_ref[...] = jnp.zeros_like(state_ref)

    r, cc = _iota2((pw_, gw))
    bd_wide = r // c == cc // dk
    zero16 = jnp.zeros((), BF16)
    zero = jnp.zeros((dk, dk), BF16)
    hsl = lambda h: slice(h * dk, (h + 1) * dk)
    units = [(bi, h) for bi in range(nseq) for h in range(heads)]
    for ci in range(nchunk):
        rows = slice(ci * c, (ci + 1) * c)
        st16 = {bh: state_ref[bh[0], bh[1]].astype(BF16) for bh in units}
        rd = {}
        for bi, h in units[::2]:
            psl = slice(h * dk, (h + 2) * dk)
            st_pair = jnp.concatenate([jnp.concatenate([st16[bi, h], zero], axis=1),
                                       jnp.concatenate([zero, st16[bi, h + 1]], axis=1)], axis=0)
            pair = _dot(jnp.concatenate([w_ref[bi, rows, psl], qe_ref[bi, rows, psl]], axis=0), st_pair)
            rd[bi, h], rd[bi, h + 1] = pair[:, :dk], pair[:, dk:]
        vn16 = {(bi, h): (u_ref[bi, rows, hsl(h)] - rd[bi, h][:c]).astype(BF16) for bi, h in units}
        intra = {}
        for bi in range(nseq):
            for g in range(ngroup):
                vg = jnp.concatenate([vn16[bi, h] for h in range(g * GDN_GROUP, (g + 1) * GDN_GROUP)], axis=1)
                bd_vn = jnp.where(bd_wide, _tile_rows(vg, GDN_GROUP), zero16)
                intra[bi, g] = _dot(attn_ref[bi, rows, g * pw_:(g + 1) * pw_], bd_vn)
        for bi, h in units:
            state_ref[bi, h] = (state_ref[bi, h] * egl_ref[bi, ci, :, hsl(h)]
                                + _dot_tn(kd_ref[bi, rows, hsl(h)], vn16[bi, h]))
        for bi, h in units:
            g, hh = divmod(h, GDN_GROUP)
            o = rd[bi, h][c:] + intra[bi, g][:, hh * dk:(hh + 1) * dk]
            o = o * lax.rsqrt(jnp.mean(o * o, axis=-1, keepdims=True) + EPS) * ng_ref[...]
            o_ref[bi, rows, hsl(h)] = (o * _silu(z_ref[bi, rows, hsl(h)])).astype(o_ref.dtype)


def _gdn_core(qkv, z, ba, conv_w, a_log, dt_bias, norm_g, nchunk=4):
    b, s, width = z.shape
    heads = width // GDN_HEAD_DIM
    ts = nchunk * CHUNK
    pad = lambda t: jnp.pad(t.reshape(1, heads), ((0, 0), (0, LANE - heads)))
    tile = lambda n: pl.BlockSpec((1, ts, n), lambda i, j: (i, j, 0))
    egl_spec = pl.BlockSpec((1, nchunk, 1, width), lambda i, j: (i, j, 0, 0))
    packed = heads * CHUNK
    u, w, qe, kd, attn, egl = pl.pallas_call(
        functools.partial(_gdn_intra_kernel, heads=heads, nchunk=nchunk),
        grid=(b, s // ts),
        in_specs=[tile(3 * width), tile(2 * LANE), _const_spec(conv_w.shape),
                  _const_spec((1, LANE)), _const_spec((1, LANE))],
        out_specs=[tile(width), tile(width), tile(width), tile(width), tile(packed), egl_spec],
        out_shape=[jax.ShapeDtypeStruct((b, s, width), F32),
                   jax.ShapeDtypeStruct((b, s, width), BF16),
                   jax.ShapeDtypeStruct((b, s, width), BF16),
                   jax.ShapeDtypeStruct((b, s, width), BF16),
                   jax.ShapeDtypeStruct((b, s, packed), BF16),
                   jax.ShapeDtypeStruct((b, s // CHUNK, 1, width), F32)],
        scratch_shapes=[pltpu.VMEM((8, 3 * width), F32)],
        compiler_params=_cparams("parallel", "arbitrary"),
        name="gdn_intra",
    )(qkv, ba, conv_w, pad(a_log), pad(dt_bias))
    nseq = 2 if b % 2 == 0 else 1
    seq_tile = lambda n: pl.BlockSpec((nseq, ts, n), lambda i, j: (i, j, 0))
    return pl.pallas_call(
        functools.partial(_gdn_seq_kernel, heads=heads, nchunk=nchunk, nseq=nseq),
        grid=(b // nseq, s // ts),
        in_specs=[seq_tile(width), seq_tile(width), seq_tile(width), seq_tile(width), seq_tile(packed),
                  pl.BlockSpec((nseq, nchunk, 1, width), lambda i, j: (i, j, 0, 0)),
                  seq_tile(width), _const_spec((1, GDN_HEAD_DIM))],
        out_specs=seq_tile(width),
        out_shape=jax.ShapeDtypeStruct((b, s, width), BF16),
        scratch_shapes=[pltpu.VMEM((nseq, heads, GDN_HEAD_DIM, GDN_HEAD_DIM), F32)],
        compiler_params=_cparams("parallel", "arbitrary"),
        name="gdn_seq",
    )(u, w, qe, kd, attn, egl, z, norm_g.reshape(1, GDN_HEAD_DIM))


def _kv_kernel(x_ref, g_ref, sh_ref, sc_ref, wdl_ref, wdr_ref, lg_ref, wk_ref, wv_ref,
               kgn_ref, kgr_ref, cos_ref, sin_ref, k_ref, vt_ref, *, heads):
    hb = _modulated(x_ref[0], g_ref[...], sh_ref[0], sc_ref[0]).astype(BF16)
    lat = _dot(hb, wdl_ref[...])
    rope = _dot(hb, wdr_ref[...])
    lat = lat * lax.rsqrt(jnp.mean(lat * lat, axis=-1, keepdims=True) + EPS) * lg_ref[...]
    lat16 = lat.astype(BF16)
    k_nope = _dot(lat16, wk_ref[...])
    v = _dot(lat16, wv_ref[...])
    rope_sq = jnp.sum(rope * rope, axis=-1, keepdims=True)
    for h in range(heads):
        sl = slice(h * QK_NOPE, (h + 1) * QK_NOPE)
        kn = k_nope[:, sl]
        inv = lax.rsqrt((jnp.sum(kn * kn, axis=-1, keepdims=True) + rope_sq) / (QK_NOPE + QK_ROPE) + EPS)
        k_ref[0, h, :, :QK_NOPE] = (kn * inv * kgn_ref[...]).astype(k_ref.dtype)
        kr = _rotate(rope * inv * kgr_ref[...], cos_ref[0], sin_ref[0])
        k_ref[0, h, :, QK_NOPE:] = kr.astype(k_ref.dtype)
        vt_ref[0, h, 0, :V_HEAD, :] = jnp.transpose(v[:, h * V_HEAD:(h + 1) * V_HEAD]).astype(vt_ref.dtype)
        vt_ref[0, h, 0, V_HEAD:, :] = jnp.ones((V_AUG - V_HEAD, vt_ref.shape[-1]), vt_ref.dtype)


def _shared_kv(x, norm_g, shift, scale, w_dl, w_dr, lat_g, w_k, w_v, kg_nope, kg_rope, cos, sin, tm=512):
    b, s, d = x.shape
    heads = w_k.shape[1] // QK_NOPE
    tile = lambda n: pl.BlockSpec((1, tm, n), lambda i, j: (i, j, 0))
    vec = pl.BlockSpec((1, 1, d), lambda i, j: (i, 0, 0))
    return pl.pallas_call(
        functools.partial(_kv_kernel, heads=heads),
        grid=(b, s // tm),
        in_specs=[tile(d), _const_spec((1, d)), vec, vec, _const_spec(w_dl.shape), _const_spec(w_dr.shape),
                  _const_spec(lat_g.shape), _const_spec(w_k.shape), _const_spec(w_v.shape),
                  _const_spec(kg_nope.shape), _const_spec(kg_rope.shape), tile(LANE), tile(LANE)],
        out_specs=[pl.BlockSpec((1, heads, tm, QK_PAD), lambda i, j: (i, 0, j, 0)),
                   pl.BlockSpec((1, heads, 1, V_AUG, tm), lambda i, j: (i, 0, j, 0, 0))],
        out_shape=[jax.ShapeDtypeStruct((b, heads, s, QK_PAD), BF16),
                   jax.ShapeDtypeStruct((b, heads, s // tm, V_AUG, tm), BF16)],
        compiler_params=_cparams("parallel", "parallel"),
        name="mla_kv",
    )(x, norm_g.reshape(1, d), shift, scale, w_dl, w_dr, lat_g, w_k, w_v, kg_nope, kg_rope, cos, sin)


def _q_kernel(x_ref, g_ref, sh_ref, sc_ref, wdq_ref, qlg_ref, wqn_ref, wqr_ref,
              qgn_ref, qgr_ref, cos_ref, sin_ref, q_ref, *, heads):
    hb = _modulated(x_ref[0], g_ref[...], sh_ref[0], sc_ref[0]).astype(BF16)
    ql = _dot(hb, wdq_ref[...])
    ql = ql * lax.rsqrt(jnp.mean(ql * ql, axis=-1, keepdims=True) + EPS) * qlg_ref[...]
    ql16 = ql.astype(BF16)
    q_nope = _dot(ql16, wqn_ref[...])
    q_rope = _dot(ql16, wqr_ref[...])
    sm_scale = (QK_NOPE + QK_ROPE) ** -0.5 * 1.4426950408889634
    qn = [q_nope[:, h * LANE:(h + 1) * LANE] for h in range(heads)]
    qr = [q_rope[:, h * LANE:(h + 1) * LANE] for h in range(heads)]
    ssq = [jnp.sum(a * a, axis=-1, keepdims=True) + jnp.sum(r * r, axis=-1, keepdims=True) for a, r in zip(qn, qr)]
    inv = [lax.rsqrt(t / (QK_NOPE + QK_ROPE) + EPS) * sm_scale for t in ssq]
    for h in range(heads):
        q_ref[0, h, :, :QK_NOPE] = (qn[h] * inv[h] * qgn_ref[...]).astype(q_ref.dtype)
    for h in range(heads):
        q_ref[0, h, :, QK_NOPE:] = _rotate(qr[h] * inv[h] * qgr_ref[...], cos_ref[0], sin_ref[0]).astype(q_ref.dtype)


def _mla_q(x, norm_g, shift, scale, w_dq, ql_g, w_qn, w_qr, qg_nope, qg_rope, cos, sin, tm=512):
    b, s, d = x.shape
    heads = w_qn.shape[1] // QK_NOPE
    tile = lambda n: pl.BlockSpec((1, tm, n), lambda i, j: (i, j, 0))
    vec = pl.BlockSpec((1, 1, d), lambda i, j: (i, 0, 0))
    return pl.pallas_call(
        functools.partial(_q_kernel, heads=heads),
        grid=(b, s // tm),
        in_specs=[tile(d), _const_spec((1, d)), vec, vec, _const_spec(w_dq.shape), _const_spec(ql_g.shape),
                  _const_spec(w_qn.shape), _const_spec(w_qr.shape), _const_spec(qg_nope.shape),
                  _const_spec(qg_rope.shape), tile(LANE), tile(LANE)],
        out_specs=pl.BlockSpec((1, heads, tm, QK_PAD), lambda i, j: (i, 0, j, 0)),
        out_shape=jax.ShapeDtypeStruct((b, heads, s, QK_PAD), BF16),
        compiler_params=_cparams("parallel", "parallel"),
        name="mla_q",
    )(x, norm_g.reshape(1, d), shift, scale, w_dq, ql_g, w_qn, w_qr, qg_nope, qg_rope, cos, sin)


def _attn_kernel(q_ref, k_ref, vt_ref, o_ref, s_ref, p_ref, m_ref, al_ref, acc_ref, *, tq):
    nq = q_ref.shape[2] // tq
    pairs = [(qi, j) for qi in range(nq) for j in range(qi + 1)]

    def scores(slot, qi, j):
        s_ref[slot] = _dot_nt(k_ref[0, 0, j * tq:(j + 1) * tq, :], q_ref[0, 0, qi * tq:(qi + 1) * tq, :])

    def softmax(slot, qi, j):
        s = s_ref[slot]
        if j == qi:
            key_c = lax.broadcasted_iota(jnp.int32, (tq, tq), 0) // CHUNK
            qry_c = lax.broadcasted_iota(jnp.int32, (tq, tq), 1) // CHUNK
            s = jnp.where(key_c <= qry_c, s, -jnp.inf)
        m_new = jnp.max(s, axis=0, keepdims=True)
        if j > 0:
            m = m_ref[...]
            m_new = jnp.maximum(m, m_new)
            al_ref[slot] = jnp.exp2(m - m_new)
        p_ref[slot] = jnp.exp2(s - m_new).astype(BF16)
        if j < qi:
            m_ref[...] = m_new

    def value(slot, qi, j):
        acc = _dot(vt_ref[0, 0, j], p_ref[slot])
        if j > 0:
            acc = al_ref[slot] * acc_ref[...] + acc
        if j == qi:
            out = acc[:V_HEAD] / acc[V_HEAD:V_HEAD + 1]
            o_ref[0, qi * tq:(qi + 1) * tq, :] = jnp.transpose(out).astype(o_ref.dtype)
        else:
            acc_ref[...] = acc

    scores(0, *pairs[0])
    for t, pair in enumerate(pairs):
        if t + 1 < len(pairs):
            scores((t + 1) % 2, *pairs[t + 1])
        if t > 0:
            value((t - 1) % 2, *pairs[t - 1])
        softmax(t % 2, *pair)
    value((len(pairs) - 1) % 2, *pairs[-1])


def _attention(q, k, vt):
    b, heads, s, dq = q.shape
    tq = vt.shape[-1]
    nkb = s // tq
    whole = lambda i, h: (i, h, 0, 0)
    return pl.pallas_call(
        functools.partial(_attn_kernel, tq=tq),
        grid=(b, heads),
        in_specs=[pl.BlockSpec((1, 1, s, dq), whole),
                  pl.BlockSpec((1, 1, s, dq), whole),
                  pl.BlockSpec((1, 1, nkb, V_AUG, tq), lambda i, h: (i, h, 0, 0, 0))],
        out_specs=pl.BlockSpec((1, s, V_HEAD), lambda i, h: (i, 0, h)),
        out_shape=jax.ShapeDtypeStruct((b, s, heads * V_HEAD), BF16),
        scratch_shapes=[pltpu.VMEM((2, tq, tq), F32), pltpu.VMEM((2, tq, tq), BF16), pltpu.VMEM((1, tq), F32),
                        pltpu.VMEM((2, 1, tq), F32), pltpu.VMEM((V_AUG, tq), F32)],
        compiler_params=_cparams("parallel", "parallel"),
        name="mla_attn",
    )(q, k, vt)


def _pad_cols(w, n):
    return jnp.pad(w, ((0, 0), (0, n - w.shape[1])))


def kernel(x, c, positions, ada_w, ada_b, norm_g, ffn_w_in, ffn_w_out, gdn_w_in, gdn_conv_w, gdn_a_log,
           gdn_dt_bias, gdn_norm_g, gdn_w_out, kv_ada_w, kv_ada_b, kv_norm_g, mla_w_dkv, mla_kv_norm_g,
           mla_w_ukv, mla_k_norm_g, mla_w_dq, mla_q_lora_norm_g, mla_w_uq, mla_q_norm_g, mla_w_out):
    b, s, d = x.shape
    depth = ada_w.shape[0]
    n_a = gdn_w_in.shape[0]
    n_mod = ada_w.shape[2] // d
    width = gdn_w_out.shape[1]
    g_heads = width // GDN_HEAD_DIM
    kv_lora = mla_kv_norm_g.shape[0]
    m_heads = mla_w_ukv.shape[1] // (QK_NOPE + V_HEAD)

    c_pad = jnp.pad(c, ((0, 8 - b), (0, 0)))
    mod = _modulation(c_pad, ada_w, ada_b)[:, :b].reshape(depth, b, n_mod, 1, d)
    kv_mod = _modulation(c_pad, kv_ada_w[None], kv_ada_b[None])[0, :b].reshape(b, 2, 1, d)
    cos, sin = _rope_tables(positions)

    ffn_order = [(l, k) for l in range(depth) for k in range(2)]
    ffn_w = {ffn_order[0]: (ffn_w_in[0, 0].astype(BF16), ffn_w_out[0, 0].astype(BF16))}

    def ffn(x, l, k, *mods, pre=None):
        pos = ffn_order.index((l, k))
        nxt = (ffn_w_in, ffn_w_out, ffn_order[pos + 1]) if pos + 1 < len(ffn_order) else None
        x, w_next = _ffn(x, norm_g[l, 2 * k], *mods, *ffn_w.pop((l, k)), nxt=nxt, pre=pre)
        if nxt is not None:
            ffn_w[nxt[2]] = w_next
        return x

    gdn_w = gdn_w_in.astype(BF16)
    gdn_wo = gdn_w_out.astype(BF16)
    mla_wo = mla_w_out.astype(BF16)
    k_sh = vt_sh = None
    for l in range(depth):
        m = lambda i: mod[l, :, i]
        x = ffn(x, l, 0, m(0), m(1), m(2))
        if l < n_a:
            w = gdn_w_in[l]
            w_ba = jnp.concatenate([_pad_cols(w[:, 4 * width:4 * width + g_heads], LANE),
                                    _pad_cols(w[:, 4 * width + g_heads:], LANE)], axis=1).astype(BF16)
            qkv, z, ba = _gdn_in(x, norm_g[l, 1], m(3), m(4), gdn_w, l, width, w_ba)
            y = _gdn_core(qkv, z, ba, gdn_conv_w[l], gdn_a_log[l], gdn_dt_bias[l], gdn_norm_g[l])
            pre = (y, gdn_wo, l, m(5))
        else:
            j = l - n_a
            w_uq = mla_w_uq[j].reshape(-1, m_heads, QK_NOPE + QK_ROPE)
            w_qn = w_uq[:, :, :QK_NOPE].reshape(-1, m_heads * QK_NOPE).astype(BF16)
            w_qr = jnp.pad(w_uq[:, :, QK_NOPE:], ((0, 0), (0, 0), (0, LANE - QK_ROPE)))
            w_qr = w_qr.reshape(-1, m_heads * LANE).astype(BF16)
            qg = mla_q_norm_g[j]
            q = _mla_q(x, norm_g[l, 1], m(3), m(4), mla_w_dq[j].astype(BF16),
                       mla_q_lora_norm_g[j].reshape(1, -1), w_qn, w_qr,
                       qg[:QK_NOPE].reshape(1, -1), _pad_cols(qg[QK_NOPE:].reshape(1, -1), LANE), cos, sin)
            y = _attention(q, k_sh, vt_sh)
            pre = (y, mla_wo, j, m(5))
        x = ffn(x, l, 1, m(6), m(7), m(8), pre=pre)
        if l == n_a - 1:
            w_ukv = mla_w_ukv.reshape(kv_lora, m_heads, QK_NOPE + V_HEAD)
            w_k = w_ukv[:, :, :QK_NOPE].reshape(kv_lora, m_heads * QK_NOPE).astype(BF16)
            w_v = w_ukv[:, :, QK_NOPE:].reshape(kv_lora, m_heads * V_HEAD).astype(BF16)
            kg = mla_k_norm_g
            k_sh, vt_sh = _shared_kv(
                x, kv_norm_g, kv_mod[:, 0], kv_mod[:, 1], mla_w_dkv[:, :kv_lora].astype(BF16),
                _pad_cols(mla_w_dkv[:, kv_lora:], LANE).astype(BF16), mla_kv_norm_g.reshape(1, -1), w_k, w_v,
                kg[:QK_NOPE].reshape(1, -1), _pad_cols(kg[QK_NOPE:].reshape(1, -1), LANE), cos, sin)
    return x
```

```python
import functools

import jax
import jax.numpy as jnp
from jax import lax
from jax.experimental import pallas as pl
from jax.experimental.pallas import tpu as pltpu

F32 = jnp.float32
BF16 = jnp.bfloat16

EPS = 1e-6
CHUNK = 64
ROPE_BASE = 10000.0
LANE = 128
VMEM_LIMIT = 56 * 1024 * 1024

GDN_HEAD_DIM = 128
QK_NOPE = 128
QK_ROPE = 64
V_HEAD = 128
V_AUG = V_HEAD + 16
QK_PAD = 256


def _cparams(*sem):
    return pltpu.CompilerParams(dimension_semantics=sem, vmem_limit_bytes=VMEM_LIMIT)


def _silu(t):
    half = 0.5 * t
    return half + half * jnp.tanh(half)


def _dot(a, b):
    return jnp.dot(a, b, preferred_element_type=F32)


def _dot_nt(a, b):
    return lax.dot_general(a, b, (((1,), (1,)), ((), ())), preferred_element_type=F32)


def _dot_tn(a, b):
    return lax.dot_general(a, b, (((0,), (0,)), ((), ())), preferred_element_type=F32)


def _split3(t):
    hi = t.astype(BF16)
    r1 = t - hi.astype(F32)
    mid = r1.astype(BF16)
    lo = (r1 - mid.astype(F32)).astype(BF16)
    return hi, mid, lo


def _const_spec(shape):
    nd = len(shape)
    return pl.BlockSpec(shape, lambda *_: (0,) * nd, pipeline_mode=pl.Buffered(1))


def _pick_spec(lead, block, tail=None):
    index = tuple(lead) + tuple(tail or (0,) * len(block))
    return pl.BlockSpec((None,) * len(lead) + tuple(block), lambda *_: index, pipeline_mode=pl.Buffered(1))


def _modulated(x, g, shift, scale):
    ms = jnp.mean(x * x, axis=-1, keepdims=True)
    return (x * lax.rsqrt(ms + EPS)) * (g * (1.0 + scale)) + shift


def _mod_kernel(c_ref, w_ref, b_ref, o_ref):
    ca = _silu(c_ref[...])
    c_hi = ca.astype(BF16)
    c_lo = (ca - c_hi.astype(F32)).astype(BF16)
    w = w_ref[0]
    w_hi = w.astype(BF16)
    w_lo = (w - w_hi.astype(F32)).astype(BF16)
    o_ref[0] = _dot(c_hi, w_hi) + (_dot(c_lo, w_hi) + _dot(c_hi, w_lo)) + b_ref[0]


def _modulation(c_pad, w, b):
    nl, d, n = w.shape
    tn = 1024
    return pl.pallas_call(
        _mod_kernel,
        grid=(nl, n // tn),
        in_specs=[
            pl.BlockSpec((8, d), lambda l, j: (0, 0)),
            pl.BlockSpec((1, d, tn), lambda l, j: (l, 0, j)),
            pl.BlockSpec((1, 1, tn), lambda l, j: (l, 0, j)),
        ],
        out_specs=pl.BlockSpec((1, 8, tn), lambda l, j: (l, 0, j)),
        out_shape=jax.ShapeDtypeStruct((nl, 8, n), F32),
        compiler_params=_cparams("parallel", "parallel"),
        name="adaln_mod",
    )(c_pad, w, b.reshape(nl, 1, n))


def _rope_kernel(pos_ref, cos_ref, sin_ref):
    half = QK_ROPE // 2
    pos = pos_ref[0].astype(F32)
    lane = lax.broadcasted_iota(jnp.int32, (1, LANE), 1)
    idx = jnp.where(lane < half, lane, lane - half).astype(F32)
    inv_freq = jnp.exp(idx * (-jnp.log(ROPE_BASE) / half))
    ang = pos * inv_freq
    valid = lane < QK_ROPE
    cos_ref[0] = jnp.where(valid, jnp.cos(ang), 0.0)
    sin_ref[0] = jnp.where(valid, jnp.where(lane < half, -jnp.sin(ang), jnp.sin(ang)), 0.0)


def _rope_tables(positions):
    b, s = positions.shape
    ts = 512
    out = jax.ShapeDtypeStruct((b, s, LANE), F32)
    return pl.pallas_call(
        _rope_kernel,
        grid=(b, s // ts),
        in_specs=[pl.BlockSpec((1, ts, 1), lambda i, j: (i, j, 0))],
        out_specs=[pl.BlockSpec((1, ts, LANE), lambda i, j: (i, j, 0))] * 2,
        out_shape=[out, out],
        compiler_params=_cparams("parallel", "parallel"),
        name="rope_tables",
    )(positions.reshape(b, s, 1))


def _rotate(t, cos, sin):
    half = QK_ROPE // 2
    lane = lax.broadcasted_iota(jnp.int32, t.shape, 1)
    swapped = jnp.where(lane < half, pltpu.roll(t, LANE - half, 1), pltpu.roll(t, half, 1))
    return t * cos + swapped * sin


def _ffn_kernel(*refs, pre):
    if pre:
        (x_ref, y_ref, wo_ref, gmix_ref, g_ref, sh_ref, sc_ref, gate_ref,
         wg_ref, wu_ref, wd_ref, out_ref) = refs
    else:
        x_ref, g_ref, sh_ref, sc_ref, gate_ref, wg_ref, wu_ref, wd_ref, out_ref = refs
    x = x_ref[0]
    if pre:
        x = x + gmix_ref[0] * _dot(y_ref[0], wo_ref[...])
    hb = _modulated(x, g_ref[...], sh_ref[0], sc_ref[0]).astype(BF16)
    f = wg_ref.shape[1]
    bounds = [0, (f // 512) * 256, f]
    y = None
    for lo, hi in zip(bounds[:-1], bounds[1:]):
        gate = _dot(hb, wg_ref[:, lo:hi])
        up = _dot(hb, wu_ref[:, lo:hi])
        part = _dot((_silu(gate) * up).astype(BF16), wd_ref[lo:hi, :])
        y = part if y is None else y + part
    out_ref[0] = x + (0.5 * gate_ref[0]) * y


def _ffn(x, norm_g, shift, scale, gate, w_in, w_out, idx, pre=None, tm=512):
    b, s, d = x.shape
    f = w_out.shape[-2]
    tile = pl.BlockSpec((1, tm, d), lambda i, j: (i, j, 0))
    vec = pl.BlockSpec((1, 1, d), lambda i, j: (i, 0, 0))
    args, specs = [x], [tile]
    if pre is not None:
        y, w_o, jo, g_mix = pre
        args += [y, w_o, g_mix]
        specs += [pl.BlockSpec((1, tm, y.shape[-1]), lambda i, j: (i, j, 0)), _pick_spec((jo,), w_o.shape[1:]), vec]
    args += [norm_g.reshape(1, d), shift, scale, gate, w_in, w_in, w_out]
    specs += [_const_spec((1, d)), vec, vec, vec,
              _pick_spec(idx, (d, f), (0, 0)), _pick_spec(idx, (d, f), (0, 1)), _pick_spec(idx, (f, d))]
    return pl.pallas_call(
        functools.partial(_ffn_kernel, pre=pre is not None),
        grid=(b, s // tm),
        in_specs=specs,
        out_specs=tile,
        out_shape=jax.ShapeDtypeStruct((b, s, d), F32),
        compiler_params=_cparams("parallel", "parallel"),
        name="ffn_pre" if pre is not None else "ffn",
    )(*args)


def _gdn_in_kernel(x_ref, g_ref, sh_ref, sc_ref, w_ref, wba_ref, qkv_ref, z_ref, ba_ref):
    hb = _modulated(x_ref[0], g_ref[...], sh_ref[0], sc_ref[0]).astype(BF16)
    width = z_ref.shape[-1]
    p = _dot(hb, w_ref[...])
    qkv_ref[0] = p[:, :3 * width]
    z_ref[0] = p[:, 3 * width:]
    ba_ref[0] = _dot(hb, wba_ref[...])


def _gdn_in(x, norm_g, shift, scale, w_all, l, width, w_ba, tm=512):
    b, s, d = x.shape
    tile = lambda n: pl.BlockSpec((1, tm, n), lambda i, j: (i, j, 0))
    vec = pl.BlockSpec((1, 1, d), lambda i, j: (i, 0, 0))
    return pl.pallas_call(
        _gdn_in_kernel,
        grid=(b, s // tm),
        in_specs=[tile(d), _const_spec((1, d)), vec, vec, _pick_spec((l,), (d, 4 * width)), _const_spec(w_ba.shape)],
        out_specs=[tile(3 * width), tile(width), tile(2 * LANE)],
        out_shape=[jax.ShapeDtypeStruct((b, s, 3 * width), F32),
                   jax.ShapeDtypeStruct((b, s, width), F32),
                   jax.ShapeDtypeStruct((b, s, 2 * LANE), F32)],
        compiler_params=_cparams("parallel", "parallel"),
        name="gdn_in",
    )(x, norm_g.reshape(1, d), shift, scale, w_all, w_ba)


def _split2(t):
    hi = t.astype(BF16)
    return hi, (t - hi.astype(F32)).astype(BF16)


def _dot2(pieces, m):
    return _dot(pieces[0], m) + _dot(pieces[1], m)


def _dot2_left(m, pieces):
    return _dot(m, pieces[0]) + _dot(m, pieces[1])


def _tile_rows(t, n):
    return jnp.concatenate([t] * n, axis=0)


def _iota2(shape):
    return lax.broadcasted_iota(jnp.int32, shape, 0), lax.broadcasted_iota(jnp.int32, shape, 1)


GDN_GROUP = 4
NEUMANN_SPLIT_STEPS = 3


def _gdn_intra_kernel(qkv_ref, ba_ref, convw_ref, alog_ref, dtb_ref,
                      u_ref, w_ref, qe_ref, kd_ref, attn_ref, egl_ref, tail_ref, *, heads, nchunk):
    c = CHUNK
    dk = GDN_HEAD_DIM
    width = heads * dk
    ts = nchunk * c
    gw = GDN_GROUP * dk
    pw_ = GDN_GROUP * c
    ngroup = heads // GDN_GROUP

    @pl.when(pl.program_id(1) == 0)
    def _():
        tail_ref[...] = jnp.zeros(tail_ref.shape, F32)

    cur = qkv_ref[0]
    prev = tail_ref[...]
    k_taps = convw_ref.shape[0]
    acc = cur * convw_ref[k_taps - 1:k_taps, :]
    cur3 = cur.reshape(ts // 8, 8, 3 * width)
    row3 = lax.broadcasted_iota(jnp.int32, (1, 8, 1), 1)
    for j in range(k_taps - 1):
        d = k_taps - 1 - j
        rot = pltpu.roll(cur3, d, 1)
        above = jnp.concatenate([pltpu.roll(prev, d, 0)[None], rot[:-1]], axis=0)
        acc = acc + jnp.where(row3 < d, above, rot).reshape(ts, 3 * width) * convw_ref[j:j + 1, :]
    tail_ref[...] = cur[ts - 8:]
    act = _silu(acc)

    def l2n(t, scale):
        parts = []
        for h in range(heads):
            th = t[:, h * dk:(h + 1) * dk]
            parts.append(th * (lax.rsqrt(jnp.sum(th * th, axis=-1, keepdims=True) + EPS) * scale))
        return jnp.concatenate(parts, axis=1)

    qn = l2n(act[:, :width], dk ** -0.5)
    kn = l2n(act[:, width:2 * width], 1.0)
    v16 = act[:, 2 * width:].astype(BF16)
    k16 = kn.astype(BF16)
    q16 = qn.astype(BF16)

    ba = ba_ref[0]
    beta = 1.0 / (1.0 + jnp.exp(-ba[:, :LANE]))
    al = ba[:, LANE:] + dtb_ref[...]
    softplus = jnp.maximum(al, 0.0) + jnp.log1p(jnp.exp(-jnp.abs(al)))
    lane = lax.broadcasted_iota(jnp.int32, (ts, LANE), 1)
    gl = jnp.where(lane < heads, -jnp.exp(alog_ref[...]) * softplus, 0.0)

    r, cc = _iota2((ts, ts))
    tri = ((r // c == cc // c) & (r >= cc)).astype(BF16)
    gcum = _dot2_left(tri, _split2(gl))
    r, cc = _iota2((LANE, width))
    e_full = (r == cc // dk).astype(BF16)
    r, cc = _iota2((LANE, heads * c))
    e_pack = (r == cc // c).astype(BF16)
    gc_pieces = _split2(gcum)
    gc_full = _dot2(gc_pieces, e_full)
    gc_pack = _dot2(gc_pieces, e_pack)
    beta_pack = _dot(beta.astype(BF16), e_pack)

    qe_ref[0] = (qn * jnp.exp(gc_full)).astype(qe_ref.dtype)

    r, cc = _iota2((c, pw_))
    eye_p = r == cc % c
    incl_p = r >= cc % c
    strict_p = r > cc % c
    r, cc = _iota2((pw_, gw))
    bd_wide = r // c == cc // dk
    r, cc = _iota2((pw_, pw_))
    bd_sq = r // c == cc // c
    zero16 = jnp.zeros((), BF16)

    units = [(ci, g) for ci in range(nchunk) for g in range(ngroup)]
    rows = lambda ci: slice(ci * c, (ci + 1) * c)
    gsl = lambda g: slice(g * gw, (g + 1) * gw)
    psl = lambda g: slice(g * pw_, (g + 1) * pw_)

    for ci in range(nchunk):
        g_last = gc_full[ci * c + c - 1:ci * c + c, :]
        egl_ref[0, ci] = jnp.exp(g_last)
        kd_ref[0, rows(ci), :] = (kn[rows(ci)] * jnp.exp(g_last - gc_full[rows(ci)])).astype(kd_ref.dtype)

    bd_k = [jnp.where(bd_wide, _tile_rows(k16[rows(ci), gsl(g)], GDN_GROUP), zero16) for ci, g in units]
    sc = [_dot_nt(jnp.concatenate([k16[rows(ci), gsl(g)], q16[rows(ci), gsl(g)]], axis=0), bd_k[i])
          for i, (ci, g) in enumerate(units)]
    a_p, beta_r, eg_r = [], [], []
    for i, (ci, g) in enumerate(units):
        gcp = gc_pack[rows(ci), psl(g)]
        gr = jnp.sum(jnp.where(eye_p, gcp, 0.0), axis=0, keepdims=True)
        bp = beta_pack[rows(ci), psl(g)]
        beta_r.append(jnp.sum(jnp.where(eye_p, bp, 0.0), axis=0, keepdims=True))
        eg_r.append(jnp.exp(gr))
        decay = jnp.where(incl_p, jnp.exp(jnp.where(incl_p, gcp - gr, 0.0)), 0.0)
        a_p.append(jnp.where(strict_p, sc[i][:c] * bp * decay, 0.0))
        attn_ref[0, rows(ci), psl(g)] = jnp.where(incl_p, sc[i][c:] * decay, 0.0).astype(attn_ref.dtype)

    def bd(t16):
        return jnp.where(bd_sq, _tile_rows(t16, GDN_GROUP), zero16)

    def prod(lhs, rhs, hi):
        lh, ll = _split2(lhs)
        rh, rl = _split2(rhs)
        bh = bd(rh)
        if not hi:
            return _dot(lh, bh)
        return _dot(lh, bh) + (_dot(ll, bh) + _dot(lh, bd(rl)))

    n_units = len(units)
    n_sq = c.bit_length() - 2
    t_inv = [jnp.where(eye_p, 1.0, 0.0) - a for a in a_p]
    pw = [prod(a_p[i], a_p[i], NEUMANN_SPLIT_STEPS > 0) for i in range(n_units)]
    for step in range(n_sq):
        hi = step < NEUMANN_SPLIT_STEPS
        if step < n_sq - 1:
            both = [prod(jnp.concatenate([t_inv[i], pw[i]], axis=0), pw[i], hi) for i in range(n_units)]
            t_inv = [t_inv[i] + both[i][:c] for i in range(n_units)]
            pw = [both[i][c:] for i in range(n_units)]
        else:
            t_inv = [t_inv[i] + prod(t_inv[i], pw[i], hi) for i in range(n_units)]

    for i, (ci, g) in enumerate(units):
        t_u = (t_inv[i] * beta_r[i]).astype(BF16)
        t_w = (t_inv[i] * (beta_r[i] * eg_r[i])).astype(BF16)
        bd_v = jnp.where(bd_wide, _tile_rows(v16[rows(ci), gsl(g)], GDN_GROUP), zero16)
        u_ref[0, rows(ci), gsl(g)] = _dot(t_u, bd_v).astype(u_ref.dtype)
        w_ref[0, rows(ci), gsl(g)] = _dot(t_w, bd_k[i]).astype(w_ref.dtype)


def _gdn_seq_kernel(u_ref, w_ref, qe_ref, kd_ref, attn_ref, egl_ref, z_ref, ng_ref, o_ref, state_ref,
                    *, heads, nchunk, nseq):
    c = CHUNK
    dk = GDN_HEAD_DIM
    gw = GDN_GROUP * dk
    pw_ = GDN_GROUP * c
    ngroup = heads // GDN_GROUP

    @pl.when(pl.program_id(1) == 0)
    def _():
        state_ref[...] = jnp.zeros_like(state_ref)

    r, cc = _iota2((pw_, gw))
    bd_wide = r // c == cc // dk
    zero16 = jnp.zeros((), BF16)
    zero = jnp.zeros((dk, dk), BF16)
    hsl = lambda h: slice(h * dk, (h + 1) * dk)
    units = [(bi, h) for bi in range(nseq) for h in range(heads)]
    for ci in range(nchunk):
        rows = slice(ci * c, (ci + 1) * c)
        st16 = {bh: state_ref[bh[0], bh[1]].astype(BF16) for bh in units}
        rd = {}
        for bi, h in units[::2]:
            psl = slice(h * dk, (h + 2) * dk)
            st_pair = jnp.concatenate([jnp.concatenate([st16[bi, h], zero], axis=1),
                                       jnp.concatenate([zero, st16[bi, h + 1]], axis=1)], axis=0)
            pair = _dot(jnp.concatenate([w_ref[bi, rows, psl], qe_ref[bi, rows, psl]], axis=0), st_pair)
            rd[bi, h], rd[bi, h + 1] = pair[:, :dk], pair[:, dk:]
        vn16 = {(bi, h): (u_ref[bi, rows, hsl(h)] - rd[bi, h][:c]).astype(BF16) for bi, h in units}
        intra = {}
        for bi in range(nseq):
            for g in range(ngroup):
                vg = jnp.concatenate([vn16[bi, h] for h in range(g * GDN_GROUP, (g + 1) * GDN_GROUP)], axis=1)
                bd_vn = jnp.where(bd_wide, _tile_rows(vg, GDN_GROUP), zero16)
                intra[bi, g] = _dot(attn_ref[bi, rows, g * pw_:(g + 1) * pw_], bd_vn)
        for bi, h in units:
            state_ref[bi, h] = (state_ref[bi, h] * egl_ref[bi, ci, :, hsl(h)]
                                + _dot_tn(kd_ref[bi, rows, hsl(h)], vn16[bi, h]))
        for bi, h in units:
            g, hh = divmod(h, GDN_GROUP)
            o = rd[bi, h][c:] + intra[bi, g][:, hh * dk:(hh + 1) * dk]
            o = o * lax.rsqrt(jnp.mean(o * o, axis=-1, keepdims=True) + EPS) * ng_ref[...]
            o_ref[bi, rows, hsl(h)] = (o * _silu(z_ref[bi, rows, hsl(h)])).astype(o_ref.dtype)


def _gdn_core(qkv, z, ba, conv_w, a_log, dt_bias, norm_g, nchunk=4):
    b, s, width = z.shape
    heads = width // GDN_HEAD_DIM
    ts = nchunk * CHUNK
    pad = lambda t: jnp.pad(t.reshape(1, heads), ((0, 0), (0, LANE - heads)))
    tile = lambda n: pl.BlockSpec((1, ts, n), lambda i, j: (i, j, 0))
    egl_spec = pl.BlockSpec((1, nchunk, 1, width), lambda i, j: (i, j, 0, 0))
    packed = heads * CHUNK
    u, w, qe, kd, attn, egl = pl.pallas_call(
        functools.partial(_gdn_intra_kernel, heads=heads, nchunk=nchunk),
        grid=(b, s // ts),
        in_specs=[tile(3 * width), tile(2 * LANE), _const_spec(conv_w.shape),
                  _const_spec((1, LANE)), _const_spec((1, LANE))],
        out_specs=[tile(width), tile(width), tile(width), tile(width), tile(packed), egl_spec],
        out_shape=[jax.ShapeDtypeStruct((b, s, width), F32),
                   jax.ShapeDtypeStruct((b, s, width), BF16),
                   jax.ShapeDtypeStruct((b, s, width), BF16),
                   jax.ShapeDtypeStruct((b, s, width), BF16),
                   jax.ShapeDtypeStruct((b, s, packed), BF16),
                   jax.ShapeDtypeStruct((b, s // CHUNK, 1, width), F32)],
        scratch_shapes=[pltpu.VMEM((8, 3 * width), F32)],
        compiler_params=_cparams("parallel", "arbitrary"),
        name="gdn_intra",
    )(qkv, ba, conv_w, pad(a_log), pad(dt_bias))
    nseq = 2 if b % 2 == 0 else 1
    seq_tile = lambda n: pl.BlockSpec((nseq, ts, n), lambda i, j: (i, j, 0))
    return pl.pallas_call(
        functools.partial(_gdn_seq_kernel, heads=heads, nchunk=nchunk, nseq=nseq),
        grid=(b // nseq, s // ts),
        in_specs=[seq_tile(width), seq_tile(width), seq_tile(width), seq_tile(width), seq_tile(packed),
                  pl.BlockSpec((nseq, nchunk, 1, width), lambda i, j: (i, j, 0, 0)),
                  seq_tile(width), _const_spec((1, GDN_HEAD_DIM))],
        out_specs=seq_tile(width),
        out_shape=jax.ShapeDtypeStruct((b, s, width), BF16),
        scratch_shapes=[pltpu.VMEM((nseq, heads, GDN_HEAD_DIM, GDN_HEAD_DIM), F32)],
        compiler_params=_cparams("parallel", "arbitrary"),
        name="gdn_seq",
    )(u, w, qe, kd, attn, egl, z, norm_g.reshape(1, GDN_HEAD_DIM))


def _kv_kernel(x_ref, g_ref, sh_ref, sc_ref, wdl_ref, wdr_ref, lg_ref, wk_ref, wv_ref,
               kgn_ref, kgr_ref, cos_ref, sin_ref, k_ref, vt_ref, *, heads):
    hb = _modulated(x_ref[0], g_ref[...], sh_ref[0], sc_ref[0]).astype(BF16)
    lat = _dot(hb, wdl_ref[...])
    rope = _dot(hb, wdr_ref[...])
    lat = lat * lax.rsqrt(jnp.mean(lat * lat, axis=-1, keepdims=True) + EPS) * lg_ref[...]
    lat16 = lat.astype(BF16)
    k_nope = _dot(lat16, wk_ref[...])
    v = _dot(lat16, wv_ref[...])
    rope_sq = jnp.sum(rope * rope, axis=-1, keepdims=True)
    for h in range(heads):
        sl = slice(h * QK_NOPE, (h + 1) * QK_NOPE)
        kn = k_nope[:, sl]
        inv = lax.rsqrt((jnp.sum(kn * kn, axis=-1, keepdims=True) + rope_sq) / (QK_NOPE + QK_ROPE) + EPS)
        k_ref[0, h, :, :QK_NOPE] = (kn * inv * kgn_ref[...]).astype(k_ref.dtype)
        kr = _rotate(rope * inv * kgr_ref[...], cos_ref[0], sin_ref[0])
        k_ref[0, h, :, QK_NOPE:] = kr.astype(k_ref.dtype)
        vt_ref[0, h, 0, :V_HEAD, :] = jnp.transpose(v[:, h * V_HEAD:(h + 1) * V_HEAD]).astype(vt_ref.dtype)
        vt_ref[0, h, 0, V_HEAD:, :] = jnp.ones((V_AUG - V_HEAD, vt_ref.shape[-1]), vt_ref.dtype)


def _shared_kv(x, norm_g, shift, scale, w_dl, w_dr, lat_g, w_k, w_v, kg_nope, kg_rope, cos, sin, tm=512):
    b, s, d = x.shape
    heads = w_k.shape[1] // QK_NOPE
    tile = lambda n: pl.BlockSpec((1, tm, n), lambda i, j: (i, j, 0))
    vec = pl.BlockSpec((1, 1, d), lambda i, j: (i, 0, 0))
    return pl.pallas_call(
        functools.partial(_kv_kernel, heads=heads),
        grid=(b, s // tm),
        in_specs=[tile(d), _const_spec((1, d)), vec, vec, _const_spec(w_dl.shape), _const_spec(w_dr.shape),
                  _const_spec(lat_g.shape), _const_spec(w_k.shape), _const_spec(w_v.shape),
                  _const_spec(kg_nope.shape), _const_spec(kg_rope.shape), tile(LANE), tile(LANE)],
        out_specs=[pl.BlockSpec((1, heads, tm, QK_PAD), lambda i, j: (i, 0, j, 0)),
                   pl.BlockSpec((1, heads, 1, V_AUG, tm), lambda i, j: (i, 0, j, 0, 0))],
        out_shape=[jax.ShapeDtypeStruct((b, heads, s, QK_PAD), BF16),
                   jax.ShapeDtypeStruct((b, heads, s // tm, V_AUG, tm), BF16)],
        compiler_params=_cparams("parallel", "parallel"),
        name="mla_kv",
    )(x, norm_g.reshape(1, d), shift, scale, w_dl, w_dr, lat_g, w_k, w_v, kg_nope, kg_rope, cos, sin)


def _q_kernel(x_ref, g_ref, sh_ref, sc_ref, wdq_ref, qlg_ref, wqn_ref, wqr_ref,
              qgn_ref, qgr_ref, cos_ref, sin_ref, q_ref, *, heads):
    hb = _modulated(x_ref[0], g_ref[...], sh_ref[0], sc_ref[0]).astype(BF16)
    ql = _dot(hb, wdq_ref[...])
    ql = ql * lax.rsqrt(jnp.mean(ql * ql, axis=-1, keepdims=True) + EPS) * qlg_ref[...]
    ql16 = ql.astype(BF16)
    q_nope = _dot(ql16, wqn_ref[...])
    q_rope = _dot(ql16, wqr_ref[...])
    sm_scale = (QK_NOPE + QK_ROPE) ** -0.5 * 1.4426950408889634
    qn = [q_nope[:, h * LANE:(h + 1) * LANE] for h in range(heads)]
    qr = [q_rope[:, h * LANE:(h + 1) * LANE] for h in range(heads)]
    ssq = [jnp.sum(a * a, axis=-1, keepdims=True) + jnp.sum(r * r, axis=-1, keepdims=True) for a, r in zip(qn, qr)]
    inv = [lax.rsqrt(t / (QK_NOPE + QK_ROPE) + EPS) * sm_scale for t in ssq]
    for h in range(heads):
        q_ref[0, h, :, :QK_NOPE] = (qn[h] * inv[h] * qgn_ref[...]).astype(q_ref.dtype)
    for h in range(heads):
        q_ref[0, h, :, QK_NOPE:] = _rotate(qr[h] * inv[h] * qgr_ref[...], cos_ref[0], sin_ref[0]).astype(q_ref.dtype)


def _mla_q(x, norm_g, shift, scale, w_dq, ql_g, w_qn, w_qr, qg_nope, qg_rope, cos, sin, tm=512):
    b, s, d = x.shape
    heads = w_qn.shape[1] // QK_NOPE
    tile = lambda n: pl.BlockSpec((1, tm, n), lambda i, j: (i, j, 0))
    vec = pl.BlockSpec((1, 1, d), lambda i, j: (i, 0, 0))
    return pl.pallas_call(
        functools.partial(_q_kernel, heads=heads),
        grid=(b, s // tm),
        in_specs=[tile(d), _const_spec((1, d)), vec, vec, _const_spec(w_dq.shape), _const_spec(ql_g.shape),
                  _const_spec(w_qn.shape), _const_spec(w_qr.shape), _const_spec(qg_nope.shape),
                  _const_spec(qg_rope.shape), tile(LANE), tile(LANE)],
        out_specs=pl.BlockSpec((1, heads, tm, QK_PAD), lambda i, j: (i, 0, j, 0)),
        out_shape=jax.ShapeDtypeStruct((b, heads, s, QK_PAD), BF16),
        compiler_params=_cparams("parallel", "parallel"),
        name="mla_q",
    )(x, norm_g.reshape(1, d), shift, scale, w_dq, ql_g, w_qn, w_qr, qg_nope, qg_rope, cos, sin)


def _attn_kernel(q_ref, k_ref, vt_ref, o_ref, s_ref, p_ref, m_ref, al_ref, acc_ref, *, tq):
    nq = q_ref.shape[2] // tq
    pairs = [(qi, j) for qi in range(nq) for j in range(qi + 1)]

    def scores(slot, qi, j):
        s_ref[slot] = _dot_nt(k_ref[0, 0, j * tq:(j + 1) * tq, :], q_ref[0, 0, qi * tq:(qi + 1) * tq, :])

    def softmax(slot, qi, j):
        s = s_ref[slot]
        if j == qi:
            key_c = lax.broadcasted_iota(jnp.int32, (tq, tq), 0) // CHUNK
            qry_c = lax.broadcasted_iota(jnp.int32, (tq, tq), 1) // CHUNK
            s = jnp.where(key_c <= qry_c, s, -jnp.inf)
        m_new = jnp.max(s, axis=0, keepdims=True)
        if j > 0:
            m = m_ref[...]
            m_new = jnp.maximum(m, m_new)
            al_ref[slot] = jnp.exp2(m - m_new)
        p_ref[slot] = jnp.exp2(s - m_new).astype(BF16)
        if j < qi:
            m_ref[...] = m_new

    def value(slot, qi, j):
        acc = _dot(vt_ref[0, 0, j], p_ref[slot])
        if j > 0:
            acc = al_ref[slot] * acc_ref[...] + acc
        if j == qi:
            out = acc[:V_HEAD] / acc[V_HEAD:V_HEAD + 1]
            o_ref[0, qi * tq:(qi + 1) * tq, :] = jnp.transpose(out).astype(o_ref.dtype)
        else:
            acc_ref[...] = acc

    scores(0, *pairs[0])
    for t, pair in enumerate(pairs):
        if t + 1 < len(pairs):
            scores((t + 1) % 2, *pairs[t + 1])
        if t > 0:
            value((t - 1) % 2, *pairs[t - 1])
        softmax(t % 2, *pair)
    value((len(pairs) - 1) % 2, *pairs[-1])


def _attention(q, k, vt):
    b, heads, s, dq = q.shape
    tq = vt.shape[-1]
    nkb = s // tq
    whole = lambda i, h: (i, h, 0, 0)
    return pl.pallas_call(
        functools.partial(_attn_kernel, tq=tq),
        grid=(b, heads),
        in_specs=[pl.BlockSpec((1, 1, s, dq), whole),
                  pl.BlockSpec((1, 1, s, dq), whole),
                  pl.BlockSpec((1, 1, nkb, V_AUG, tq), lambda i, h: (i, h, 0, 0, 0))],
        out_specs=pl.BlockSpec((1, s, V_HEAD), lambda i, h: (i, 0, h)),
        out_shape=jax.ShapeDtypeStruct((b, s, heads * V_HEAD), BF16),
        scratch_shapes=[pltpu.VMEM((2, tq, tq), F32), pltpu.VMEM((2, tq, tq), BF16), pltpu.VMEM((1, tq), F32),
                        pltpu.VMEM((2, 1, tq), F32), pltpu.VMEM((V_AUG, tq), F32)],
        compiler_params=_cparams("parallel", "parallel"),
        name="mla_attn",
    )(q, k, vt)


def _pad_cols(w, n):
    return jnp.pad(w, ((0, 0), (0, n - w.shape[1])))


def kernel(x, c, positions, ada_w, ada_b, norm_g, ffn_w_in, ffn_w_out, gdn_w_in, gdn_conv_w, gdn_a_log,
           gdn_dt_bias, gdn_norm_g, gdn_w_out, kv_ada_w, kv_ada_b, kv_norm_g, mla_w_dkv, mla_kv_norm_g,
           mla_w_ukv, mla_k_norm_g, mla_w_dq, mla_q_lora_norm_g, mla_w_uq, mla_q_norm_g, mla_w_out):
    b, s, d = x.shape
    depth = ada_w.shape[0]
    n_a = gdn_w_in.shape[0]
    n_mod = ada_w.shape[2] // d
    width = gdn_w_out.shape[1]
    g_heads = width // GDN_HEAD_DIM
    kv_lora = mla_kv_norm_g.shape[0]
    m_heads = mla_w_ukv.shape[1] // (QK_NOPE + V_HEAD)

    c_pad = jnp.pad(c, ((0, 8 - b), (0, 0)))
    mod = _modulation(c_pad, ada_w, ada_b)[:, :b].reshape(depth, b, n_mod, 1, d)
    kv_mod = _modulation(c_pad, kv_ada_w[None], kv_ada_b[None])[0, :b].reshape(b, 2, 1, d)
    cos, sin = _rope_tables(positions)

    w_in = ffn_w_in.astype(BF16)
    w_out = ffn_w_out.astype(BF16)
    gdn_w = gdn_w_in.astype(BF16)
    gdn_wo = gdn_w_out.astype(BF16)
    mla_wo = mla_w_out.astype(BF16)
    k_sh = vt_sh = None
    for l in range(depth):
        m = lambda i: mod[l, :, i]
        x = _ffn(x, norm_g[l, 0], m(0), m(1), m(2), w_in, w_out, (l, 0))
        if l < n_a:
            w = gdn_w_in[l]
            w_ba = jnp.concatenate([_pad_cols(w[:, 4 * width:4 * width + g_heads], LANE),
                                    _pad_cols(w[:, 4 * width + g_heads:], LANE)], axis=1).astype(BF16)
            qkv, z, ba = _gdn_in(x, norm_g[l, 1], m(3), m(4), gdn_w, l, width, w_ba)
            y = _gdn_core(qkv, z, ba, gdn_conv_w[l], gdn_a_log[l], gdn_dt_bias[l], gdn_norm_g[l])
            pre = (y, gdn_wo, l, m(5))
        else:
            j = l - n_a
            w_uq = mla_w_uq[j].reshape(-1, m_heads, QK_NOPE + QK_ROPE)
            w_qn = w_uq[:, :, :QK_NOPE].reshape(-1, m_heads * QK_NOPE).astype(BF16)
            w_qr = jnp.pad(w_uq[:, :, QK_NOPE:], ((0, 0), (0, 0), (0, LANE - QK_ROPE)))
            w_qr = w_qr.reshape(-1, m_heads * LANE).astype(BF16)
            qg = mla_q_norm_g[j]
            q = _mla_q(x, norm_g[l, 1], m(3), m(4), mla_w_dq[j].astype(BF16),
                       mla_q_lora_norm_g[j].reshape(1, -1), w_qn, w_qr,
                       qg[:QK_NOPE].reshape(1, -1), _pad_cols(qg[QK_NOPE:].reshape(1, -1), LANE), cos, sin)
            y = _attention(q, k_sh, vt_sh)
            pre = (y, mla_wo, j, m(5))
        x = _ffn(x, norm_g[l, 2], m(6), m(7), m(8), w_in, w_out, (l, 1), pre=pre)
        if l == n_a - 1:
            w_ukv = mla_w_ukv.reshape(kv_lora, m_heads, QK_NOPE + V_HEAD)
            w_k = w_ukv[:, :, :QK_NOPE].reshape(kv_lora, m_heads * QK_NOPE).astype(BF16)
            w_v = w_ukv[:, :, QK_NOPE:].reshape(kv_lora, m_heads * V_HEAD).astype(BF16)
            kg = mla_k_norm_g
            k_sh, vt_sh = _shared_kv(
                x, kv_norm_g, kv_mod[:, 0], kv_mod[:, 1], mla_w_dkv[:, :kv_lora].astype(BF16),
                _pad_cols(mla_w_dkv[:, kv_lora:], LANE).astype(BF16), mla_kv_norm_g.reshape(1, -1), w_k, w_v,
                kg[:QK_NOPE].reshape(1, -1), _pad_cols(kg[QK_NOPE:].reshape(1, -1), LANE), cos, sin)
    return x
```
